```python
import math
import jax, jax.numpy as jnp
from jax import lax
import numpy as np

D_MODEL = 1024
BATCH = 4
SEQ = 4096
DEPTH = 4
DEC_BATCH = 16
DEC_SEQ = 4096
PAST_LEN = 128

F32 = jnp.float32
NORM_EPS = 1e-6
NEG_INF = -1e30

ATT_HEAD_DIM = 64
ATT_WIDTH = D_MODEL // 2
ATT_HEADS = ATT_WIDTH // ATT_HEAD_DIM
RWKV_HEAD_DIM = 64
RWKV_WIDTH = D_MODEL - ATT_WIDTH
RWKV_HEADS = RWKV_WIDTH // RWKV_HEAD_DIM
MIX_WIDTH = ATT_WIDTH + RWKV_WIDTH

DILATIONS = ((128, 1), (512, 4), (2048, 16))
ROPE_THETA = 500000.0
ROPE_DIM = ATT_HEAD_DIM // 4

DECAY_LORA = 64
ICLR_LORA = 64
GATE_LORA = 128
LN_X_EPS = 64e-5
RWKV_COLS = 3 * RWKV_WIDTH + 2 * DECAY_LORA + 2 * ICLR_LORA + GATE_LORA
IN_COLS = 3 * ATT_WIDTH + RWKV_COLS

PEER_HEADS = 8
PEER_N_KEYS = 128
PEER_N_EXPERTS = PEER_N_KEYS * PEER_N_KEYS
PEER_KEY_DIM = 256
PEER_TOPK = 16
PEER_BLOCK = 256

PLE_DIM = 256

kernel_name = 'hybrid_dilated_attn_rwkv7_peer_encoder'


def rms_norm(x, g):
    xf = x.astype(F32)
    y = xf * lax.rsqrt(jnp.mean(xf * xf, axis=-1, keepdims=True) + NORM_EPS)
    return (y * g.astype(F32)).astype(x.dtype)


def partial_rotary(t):
    s = t.shape[1]
    half = ROPE_DIM // 2
    inv_freq = ROPE_THETA ** (-jnp.arange(half, dtype=F32) / half)
    ang = jnp.arange(s, dtype=F32)[:, None] * inv_freq[None, :]
    cos = jnp.cos(ang)[None, :, None, :]
    sin = jnp.sin(ang)[None, :, None, :]
    tf = t.astype(F32)
    x1, x2, rest = tf[..., :half], tf[..., half:ROPE_DIM], tf[..., ROPE_DIM:]
    out = jnp.concatenate([x1 * cos - x2 * sin, x2 * cos + x1 * sin, rest], axis=-1)
    return out.astype(t.dtype)


def dilated_band_partial(q, k, v, window, dilation):
    b, s, h, dh = q.shape
    half = window // (2 * dilation)
    blk = half
    n_len = s // dilation
    nb = -(-n_len // blk)
    lp = nb * blk
    n = b * dilation

    def to_classes(t):
        t = jnp.moveaxis(t.reshape(b, n_len, dilation, h, dh), 2, 1).reshape(n, n_len, h, dh)
        t = jnp.pad(t, ((0, 0), (0, lp - n_len), (0, 0), (0, 0)))
        return t.reshape(n, nb, blk, h, dh)

    def band(t):
        tp = jnp.pad(t, ((0, 0), (1, 1), (0, 0), (0, 0), (0, 0)))
        return jnp.concatenate([tp[:, :-2], tp[:, 1:-1], tp[:, 2:]], axis=2)

    def from_classes(t):
        rest = t.shape[3:]
        t = t.reshape((b, dilation, lp) + rest)[:, :, :n_len]
        return jnp.moveaxis(t, 1, 2).reshape((b, s) + rest)

    qb = to_classes(q)
    kb = band(to_classes(k))
    vb = band(to_classes(v))
    scores = jnp.einsum('nbqhd,nbkhd->nbhqk', qb, kb).astype(F32) * (dh ** -0.5)
    qpos = jnp.arange(nb)[:, None] * blk + jnp.arange(blk)[None, :]
    kpos = jnp.arange(nb)[:, None] * blk - blk + jnp.arange(3 * blk)[None, :]
    rel = kpos[:, None, :] - qpos[:, :, None]
    valid = (jnp.abs(rel) <= half) & (kpos[:, None, :] >= 0) & (kpos[:, None, :] < n_len)
    scores = jnp.where(valid[None, :, None], scores, NEG_INF)
    m = scores.max(axis=-1)
    p = jnp.exp(scores - m[..., None])
    l = p.sum(axis=-1)
    o = jnp.einsum('nbhqk,nbkhd->nbqhd', p, vb.astype(F32))
    m = jnp.swapaxes(m, 2, 3)
    l = jnp.swapaxes(l, 2, 3)
    return from_classes(o), from_classes(m), from_classes(l)


def dilated_attention(q, k, v, norm_g):
    b, s, _ = q.shape
    heads = lambda t: t.reshape(b, s, ATT_HEADS, ATT_HEAD_DIM)
    q = partial_rotary(heads(q))
    k = partial_rotary(heads(k))
    v = heads(v)
    parts = [dilated_band_partial(q, k, v, w, d) for w, d in DILATIONS]
    m_max = jnp.max(jnp.stack([pm for _, pm, _ in parts]), axis=0)
    num = sum(jnp.exp(pm - m_max)[..., None] * po for po, pm, _ in parts)
    den = sum(jnp.exp(pm - m_max) * pl for _, pm, pl in parts)
    o = num / den[..., None]
    o = o * lax.rsqrt(jnp.mean(o * o, axis=-1, keepdims=True) + NORM_EPS)
    return (o.reshape(b, s, ATT_WIDTH) * norm_g).astype(q.dtype)


def centred_shift(u, w):
    prev = jnp.pad(u[:, :-1], ((0, 0), (1, 0), (0, 0)))
    nxt = jnp.pad(u[:, 1:], ((0, 0), (0, 1), (0, 0)))
    return w[0] * prev + w[1] * u + w[2] * nxt


def wkv_scan(r, w, k, v, a, b, reverse):
    bsz, _, h, n = r.shape
    seq = tuple(t.astype(F32).transpose(1, 0, 2, 3) for t in (r, w, k, v, a, b))

    def step(state, inp):
        r_t, w_t, k_t, v_t, a_t, b_t = inp
        sa = jnp.einsum('bhvk,bhk->bhv', state, a_t)
        state = state * w_t[:, :, None, :] + sa[..., None] * b_t[:, :, None, :] + v_t[..., None] * k_t[:, :, None, :]
        return state, jnp.einsum('bhvk,bhk->bhv', state, r_t)

    _, ys = lax.scan(step, jnp.zeros((bsz, h, n, n), F32), seq, reverse=reverse)
    return ys.transpose(1, 0, 2, 3)


def rwkv_time_mix(u, shift_w, w0, w2, a0, a2, g2, k_k, k_a, r_k, ln_w, ln_b):
    bsz, s, _ = u.shape
    u = centred_shift(u, shift_w)
    sizes = [RWKV_WIDTH, RWKV_WIDTH, RWKV_WIDTH, DECAY_LORA, DECAY_LORA, ICLR_LORA, ICLR_LORA]
    offsets = []
    acc = 0
    for sz in sizes:
        acc += sz
        offsets.append(acc)
    r, k, v, wd_f, wd_b, ad_f, ad_b, gd = jnp.split(u, offsets, axis=-1)
    g = jax.nn.sigmoid(gd) @ g2
    heads = lambda t: t.reshape(bsz, s, RWKV_HEADS, RWKV_HEAD_DIM)
    kk = heads(k * k_k).astype(F32)
    kk = kk / jnp.maximum(jnp.linalg.norm(kk, axis=-1, keepdims=True), 1e-12)

    def direction(wd, ad, d, reverse):
        w_log = -jax.nn.softplus(-(w0[d] + jnp.tanh(wd) @ w2[d])) - 0.5
        decay = jnp.exp(-jnp.exp(w_log.astype(F32)))
        a = jax.nn.sigmoid(a0[d] + ad @ a2[d])
        kd = k * (1.0 + (a - 1.0) * k_a)
        y = wkv_scan(heads(r), heads(decay), heads(kd), heads(v), -kk, kk * heads(a).astype(F32), reverse)
        return y, kd

    y_f, k_f = direction(wd_f, ad_f, 0, False)
    y_b, k_b = direction(wd_b, ad_b, 1, True)
    y = y_f + y_b
    mu = jnp.mean(y, axis=-1, keepdims=True)
    var = jnp.mean(jnp.square(y - mu), axis=-1, keepdims=True)
    y = (y - mu) * lax.rsqrt(var + LN_X_EPS)
    y = y.reshape(bsz, s, RWKV_WIDTH) * ln_w + ln_b
    bonus = jnp.sum(heads(r * (k_f + k_b)) * r_k, axis=-1, keepdims=True) * heads(v)
    return (y.astype(u.dtype) + bonus.reshape(bsz, s, RWKV_WIDTH)) * g


def peer_ffn(x, w_q, sub_keys, u_tab, v_tab):
    bsz, s, d = x.shape
    t = bsz * s
    xt = x.reshape(t, d)
    q = (xt @ w_q).reshape(t, PEER_HEADS, 2, PEER_KEY_DIM // 2)
    scores = jnp.einsum('thpc,hpnc->thpn', q, sub_keys).astype(F32)
    top_s, top_i = lax.top_k(scores, PEER_TOPK)
    cand_s = (top_s[:, :, 0, :, None] + top_s[:, :, 1, None, :]).reshape(t, PEER_HEADS, PEER_TOPK * PEER_TOPK)
    cand_i = (top_i[:, :, 0, :, None] * PEER_N_KEYS + top_i[:, :, 1, None, :]).reshape(t, PEER_HEADS, PEER_TOPK * PEER_TOPK)
    best_s, pos = lax.top_k(cand_s, PEER_TOPK)
    expert = jnp.take_along_axis(cand_i, pos, axis=-1)
    gate = jax.nn.softmax(best_s, axis=-1)
    chunk = math.gcd(t, PEER_BLOCK)
    nc = t // chunk

    def expert_block(args):
        xc, ec, gc = args
        u = jnp.take(u_tab, ec, axis=0)
        hid = jax.nn.gelu(jnp.einsum('chkd,cd->chk', u, xc).astype(F32), approximate=False)
        vv = jnp.take(v_tab, ec, axis=0)
        return jnp.einsum('chk,chkd->cd', (gc * hid).astype(x.dtype), vv)

    out = lax.map(expert_block, (xt.reshape(nc, chunk, d),
                                 expert.reshape(nc, chunk, PEER_HEADS, PEER_TOPK),
                                 gate.reshape(nc, chunk, PEER_HEADS, PEER_TOPK)))
    return out.reshape(bsz, s, d)


def encoder_layer(h, p_i, norm_mix_g, w_in, att_norm_g, rwkv_shift, rwkv_w0, rwkv_w2, rwkv_a0, rwkv_a2,
                  rwkv_g2, rwkv_k_k, rwkv_k_a, rwkv_r_k, rwkv_ln_w, rwkv_ln_b, w_out, norm_ffn_g,
                  peer_w_q, peer_sub_keys, peer_u, peer_v, ple_norm_g, ple_w_gate, ple_b_gate, ple_w_proj):
    xn = rms_norm(h, norm_mix_g)
    proj = xn @ w_in
    q, k, v = jnp.split(proj[..., :3 * ATT_WIDTH], 3, axis=-1)
    att = dilated_attention(q, k, v, att_norm_g)
    rw = rwkv_time_mix(proj[..., 3 * ATT_WIDTH:], rwkv_shift, rwkv_w0, rwkv_w2, rwkv_a0, rwkv_a2,
                       rwkv_g2, rwkv_k_k, rwkv_k_a, rwkv_r_k, rwkv_ln_w, rwkv_ln_b)
    h = h + jnp.concatenate([att, rw], axis=-1) @ w_out
    h = h + peer_ffn(rms_norm(h, norm_ffn_g), peer_w_q, peer_sub_keys, peer_u, peer_v)
    gate = jax.nn.sigmoid(rms_norm(h, ple_norm_g) @ ple_w_gate + ple_b_gate)
    return h + gate * (p_i @ ple_w_proj)


def encoder_trunk(x, p, weights, final_norm_g):
    h = x
    for i in range(DEPTH):
        h = encoder_layer(h, p[i], *[w[i] for w in weights])
    return rms_norm(h, final_norm_g)


def setup_inputs(seed: int = 0) -> dict:
    key = jax.random.key(seed)
    keys = jax.random.split(key, 32)

    def nrm(i, shape, scale):
        return jax.random.normal(keys[i], shape, F32) * scale

    def gain(i, shape):
        return 1.0 + nrm(i, shape, 0.05)

    shift_base = jnp.array([0.25, 0.5, 0.25], F32)[None, :, None]
    return {
        'x_prompt': nrm(0, (BATCH, SEQ, D_MODEL), 1.0),
        'x_sample': nrm(1, (DEC_BATCH, DEC_SEQ, D_MODEL), 1.0),
        'p_prompt': nrm(2, (DEPTH, BATCH, SEQ, PLE_DIM), 1.0),
        'p_sample': nrm(3, (DEPTH, DEC_BATCH, DEC_SEQ, PLE_DIM), 1.0),
        'norm_mix_g': gain(4, (DEPTH, D_MODEL)),
        'w_in': nrm(5, (DEPTH, D_MODEL, IN_COLS), D_MODEL ** -0.5),
        'att_norm_g': gain(6, (DEPTH, ATT_WIDTH)),
        'rwkv_shift': shift_base + nrm(7, (DEPTH, 3, RWKV_COLS), 0.1),
        'rwkv_w0': -0.5 + nrm(8, (DEPTH, 2, RWKV_WIDTH), 0.5),
        'rwkv_w2': nrm(9, (DEPTH, 2, DECAY_LORA, RWKV_WIDTH), 0.1),
        'rwkv_a0': nrm(10, (DEPTH, 2, RWKV_WIDTH), 0.5),
        'rwkv_a2': nrm(11, (DEPTH, 2, ICLR_LORA, RWKV_WIDTH), ICLR_LORA ** -0.5),
        'rwkv_g2': nrm(12, (DEPTH, GATE_LORA, RWKV_WIDTH), GATE_LORA ** -0.5),
        'rwkv_k_k': 0.85 + nrm(13, (DEPTH, RWKV_WIDTH), 0.1),
        'rwkv_k_a': gain(14, (DEPTH, RWKV_WIDTH)),
        'rwkv_r_k': nrm(15, (DEPTH, RWKV_HEADS, RWKV_HEAD_DIM), 0.1),
        'rwkv_ln_w': gain(16, (DEPTH, RWKV_WIDTH)),
        'rwkv_ln_b': nrm(17, (DEPTH, RWKV_WIDTH), 0.02),
        'w_out': nrm(18, (DEPTH, MIX_WIDTH, D_MODEL), MIX_WIDTH ** -0.5),
        'norm_ffn_g': gain(19, (DEPTH, D_MODEL)),
        'peer_w_q': nrm(20, (DEPTH, D_MODEL, PEER_HEADS * PEER_KEY_DIM), D_MODEL ** -0.5),
        'peer_sub_keys': nrm(21, (DEPTH, PEER_HEADS, 2, PEER_N_KEYS, PEER_KEY_DIM // 2), (PEER_KEY_DIM // 2) ** -0.5),
        'peer_u': nrm(22, (DEPTH, PEER_N_EXPERTS, D_MODEL), D_MODEL ** -0.5),
        'peer_v': nrm(23, (DEPTH, PEER_N_EXPERTS, D_MODEL), 0.2),
        'ple_norm_g': gain(24, (DEPTH, D_MODEL)),
        'ple_w_gate': nrm(25, (DEPTH, D_MODEL, D_MODEL), D_MODEL ** -0.5),
        'ple_b_gate': nrm(26, (DEPTH, D_MODEL), 0.02),
        'ple_w_proj': nrm(27, (DEPTH, PLE_DIM, D_MODEL), PLE_DIM ** -0.5),
        'final_norm_g': gain(28, (D_MODEL,)),
    }


def reference(x_prompt, x_sample, p_prompt, p_sample, norm_mix_g, w_in, att_norm_g, rwkv_shift,
              rwkv_w0, rwkv_w2, rwkv_a0, rwkv_a2, rwkv_g2, rwkv_k_k, rwkv_k_a, rwkv_r_k, rwkv_ln_w,
              rwkv_ln_b, w_out, norm_ffn_g, peer_w_q, peer_sub_keys, peer_u, peer_v, ple_norm_g,
              ple_w_gate, ple_b_gate, ple_w_proj, final_norm_g):
    weights = (norm_mix_g, w_in, att_norm_g, rwkv_shift, rwkv_w0, rwkv_w2, rwkv_a0, rwkv_a2, rwkv_g2,
               rwkv_k_k, rwkv_k_a, rwkv_r_k, rwkv_ln_w, rwkv_ln_b, w_out, norm_ffn_g, peer_w_q,
               peer_sub_keys, peer_u, peer_v, ple_norm_g, ple_w_gate, ple_b_gate, ple_w_proj)
    y_prompt = encoder_trunk(x_prompt, p_prompt, weights, final_norm_g)
    y_sample = encoder_trunk(x_sample, p_sample, weights, final_norm_g)
    return (y_prompt, y_sample)
```

```python
import functools

import jax
import jax.numpy as jnp
import numpy as np
from jax import lax
from jax.experimental import pallas as pl
from jax.experimental.pallas import tpu as pltpu

F32 = jnp.float32
BF16 = jnp.bfloat16

D_MODEL = 1024
HEAD_DIM = 64
ATT_WIDTH = 512
ATT_HEADS = 8
RWKV_WIDTH = 512
RWKV_COLS = 1920
QKV_COLS = 3 * ATT_WIDTH
DILATIONS = (1, 4, 16)
HALF_WINDOW = 64
ROPE_THETA = 500000.0
ROPE_DIM = 16
NORM_EPS = 1e-6
LN_X_EPS = 64e-5
NEG_INF = -1e30
PEER_HEADS = 8
PEER_N_KEYS = 128
PEER_TOPK = 16
PLE_DIM = 256
CHUNK = 64
GROUP = 256

VMEM_LIMIT = 56 * 1024 * 1024


def _cparams(*sem):
    return pltpu.CompilerParams(dimension_semantics=sem, vmem_limit_bytes=VMEM_LIMIT)


def _full(shape):
    nd = len(shape)
    return pl.BlockSpec(shape, lambda *_: (0,) * nd)


def _dot(a, b):
    return jnp.dot(a, b, preferred_element_type=F32)


def _dot_nt(a, b):
    return lax.dot_general(a, b, (((1,), (1,)), ((), ())), preferred_element_type=F32)


def _dot_tn(a, b):
    return lax.dot_general(a, b, (((0,), (0,)), ((), ())), preferred_element_type=F32)


def _split3(x):
    hi = x.astype(BF16)
    r1 = x - hi.astype(F32)
    mid = r1.astype(BF16)
    lo = (r1 - mid.astype(F32)).astype(BF16)
    return hi, mid, lo


def _dot_lexact(l_bf16, x):
    hi, mid, lo = _split3(x)
    return _dot(l_bf16, hi) + _dot(l_bf16, mid) + _dot(l_bf16, lo)


def _dot_rexact(x, r_bf16):
    hi, mid, lo = _split3(x)
    return _dot(hi, r_bf16) + _dot(mid, r_bf16) + _dot(lo, r_bf16)


def _rms(x, g):
    return x * lax.rsqrt(jnp.mean(x * x, axis=-1, keepdims=True) + NORM_EPS) * g


def _sigmoid(x):
    return 1.0 / (1.0 + jnp.exp(-x))


def _inproj_kernel(h_ref, g_ref, w_ref, cos_ref, sin_ref, q_ref, k_ref, v_ref, u_ref):
    xn = _rms(h_ref[...], g_ref[...]).astype(BF16)
    qkv = _dot(xn, w_ref[:, :QKV_COLS])
    u_ref[...] = _dot(xn, w_ref[:, QKV_COLS:])
    cos = cos_ref[...]
    sin = sin_ref[...]
    lane = lax.broadcasted_iota(jnp.int32, cos.shape, 1) % HEAD_DIM
    half = ROPE_DIM // 2

    def rot(t):
        partner = jnp.where(lane < half, pltpu.roll(t, ATT_WIDTH - half, 1), pltpu.roll(t, half, 1))
        return t * cos + partner * sin

    q_ref[...] = (rot(qkv[:, :ATT_WIDTH]) * (HEAD_DIM ** -0.5)).astype(BF16)
    k_ref[...] = rot(qkv[:, ATT_WIDTH:2 * ATT_WIDTH]).astype(BF16)
    v_ref[...] = qkv[:, 2 * ATT_WIDTH:].astype(BF16)


def _inproj(h, g, w_bf16, cos_t, sin_t, seq, tm=512):
    t = h.shape[0]
    tm = min(tm, seq)
    nseq = seq // tm
    row = lambda i: (i, 0)
    pos = lambda i: (i % nseq, 0)
    return pl.pallas_call(
        _inproj_kernel,
        grid=(t // tm,),
        in_specs=[pl.BlockSpec((tm, D_MODEL), row), _full((1, D_MODEL)), _full(w_bf16.shape),
                  pl.BlockSpec((tm, ATT_WIDTH), pos), pl.BlockSpec((tm, ATT_WIDTH), pos)],
        out_specs=[pl.BlockSpec((tm, ATT_WIDTH), row)] * 3 + [pl.BlockSpec((tm, RWKV_COLS), row)],
        out_shape=[jax.ShapeDtypeStruct((t, ATT_WIDTH), BF16)] * 3 + [jax.ShapeDtypeStruct((t, RWKV_COLS), F32)],
        compiler_params=_cparams("parallel"),
        name="inproj",
    )(h, g, w_bf16, cos_t, sin_t)


def _rope_tables(seq):
    half = ROPE_DIM // 2
    inv_freq = ROPE_THETA ** (-jnp.arange(half, dtype=F32) / half)
    ang = jnp.arange(seq, dtype=F32)[:, None] * inv_freq[None, :]
    ones = jnp.ones((seq, HEAD_DIM - ROPE_DIM), F32)
    cos_h = jnp.concatenate([jnp.cos(ang), jnp.cos(ang), ones], axis=1)
    sin_h = jnp.concatenate([-jnp.sin(ang), jnp.sin(ang), 0.0 * ones], axis=1)
    return jnp.tile(cos_h, (1, ATT_HEADS)), jnp.tile(sin_h, (1, ATT_HEADS))


ATT_SUB = 128


def _attn_kernel(q_ref, kp_ref, kc_ref, kn_ref, vp_ref, vc_ref, vn_ref, o_ref, lse_ref,
                 kbuf, vbuf, *, length, tq):
    j = pl.program_id(2)
    hw = HALF_WINDOW
    kbuf[0:hw] = kp_ref[0]
    kbuf[hw:hw + tq] = kc_ref[0]
    kbuf[hw + tq:] = kn_ref[0]
    vbuf[0:hw] = vp_ref[0]
    vbuf[hw:hw + tq] = vc_ref[0]
    vbuf[hw + tq:] = vn_ref[0]
    nk = ATT_SUB + 2 * hw
    qi = lax.broadcasted_iota(jnp.int32, (ATT_SUB, nk), 0)
    ki = lax.broadcasted_iota(jnp.int32, (ATT_SUB, nk), 1)
    band = jnp.abs(ki - hw - qi) <= hw
    for s in range(tq // ATT_SUB):
        q0 = s * ATT_SUB
        kpos = j * tq + (q0 - hw) + ki
        valid = band & (kpos >= 0) & (kpos < length)
        for h in range(ATT_HEADS):
            cols = slice(h * HEAD_DIM, (h + 1) * HEAD_DIM)
            sc = _dot_nt(q_ref[0, q0:q0 + ATT_SUB, cols], kbuf[q0:q0 + nk, cols])
            sc = jnp.where(valid, sc, NEG_INF)
            m = jnp.max(sc, axis=-1, keepdims=True)
            p = jnp.exp(sc - m)
            l = jnp.sum(p, axis=-1, keepdims=True)
            o = _dot(p.astype(BF16), vbuf[q0:q0 + nk, cols])
            o_ref[0, q0:q0 + ATT_SUB, cols] = o / l
            lse_ref[0, q0:q0 + ATT_SUB, cols] = jnp.broadcast_to(m + jnp.log(l), (ATT_SUB, HEAD_DIM))


def _attn_pattern(q, k, v, bsz, seq, dil):
    length = seq // dil
    tq = min(length, 512)
    nq = length // tq
    nh = length // HALF_WINDOW
    rq = tq // HALF_WINDOW
    view = lambda a: a.reshape(bsz, length, dil * ATT_WIDTH)
    cur = pl.BlockSpec((1, tq, ATT_WIDTH), lambda b, r, j: (b, j, r))
    prev = pl.BlockSpec((1, HALF_WINDOW, ATT_WIDTH), lambda b, r, j: (b, jnp.maximum(j * rq - 1, 0), r))
    nxt = pl.BlockSpec((1, HALF_WINDOW, ATT_WIDTH), lambda b, r, j: (b, jnp.minimum((j + 1) * rq, nh - 1), r))
    out_sd = jax.ShapeDtypeStruct((bsz, length, dil * ATT_WIDTH), F32)
    o, lse = pl.pallas_call(
        functools.partial(_attn_kernel, length=length, tq=tq),
        grid=(bsz, dil, nq),
        in_specs=[cur, prev, cur, nxt, prev, cur, nxt],
        out_specs=[cur, cur],
        out_shape=[out_sd, out_sd],
        scratch_shapes=[pltpu.VMEM((tq + 2 * HALF_WINDOW, ATT_WIDTH), BF16)] * 2,
        compiler_params=_cparams("parallel", "parallel", "parallel"),
        name=f"attn_d{dil}",
    )(view(q), view(k), view(k), view(k), view(v), view(v), view(v))
    t = bsz * seq
    return o.reshape(t, ATT_WIDTH), lse.reshape(t, ATT_WIDTH)


R_TILE = 256
R_COLS = dict(r=0, k=512, v=1024, wd=1536, ad=1664, gd=1792)


def _rwkv_prep_kernel(uc_ref, up_ref, un_ref, shift_ref, w0_ref, w2_ref, a0_ref, a2_ref, g2_ref,
                      kk_ref, ka_ref, rk_ref, bd_ref, tri_ref, sel_ref,
                      rt_ref, at_ref, bt_ref, kt_ref, bh_ref, kh_ref, v_ref, wdec_ref, bonus_ref, g_ref):
    j = pl.program_id(1)
    nj = pl.num_programs(1)
    tr = R_TILE
    cur = uc_ref[0]
    row = lax.broadcasted_iota(jnp.int32, cur.shape, 0)
    prev_edge = up_ref[0, 7:8, :] * (j > 0).astype(F32)
    next_edge = un_ref[0, 0:1, :] * (j < nj - 1).astype(F32)
    prev = jnp.where(row == 0, prev_edge, pltpu.roll(cur, 1, 0))
    nxt = jnp.where(row == tr - 1, next_edge, pltpu.roll(cur, tr - 1, 0))
    us = shift_ref[0:1, :] * prev + shift_ref[1:2, :] * cur + shift_ref[2:3, :] * nxt

    w = RWKV_WIDTH
    r = us[:, R_COLS["r"]:R_COLS["r"] + w]
    k = us[:, R_COLS["k"]:R_COLS["k"] + w]
    v = us[:, R_COLS["v"]:R_COLS["v"] + w]
    gd = us[:, R_COLS["gd"]:]
    bd = bd_ref[...]

    g_ref[...] = _dot(_sigmoid(gd).astype(BF16), g2_ref[...])
    v_ref[...] = v.astype(BF16)
    kk = k * kk_ref[...]
    kk = kk / jnp.maximum(jnp.sqrt(_dot_rexact(kk * kk, bd)), 1e-12)

    nchunk = tr // CHUNK
    ksum = jnp.zeros_like(k)
    for d in range(2):
        wd = us[:, R_COLS["wd"] + 64 * d:R_COLS["wd"] + 64 * (d + 1)]
        ad = us[:, R_COLS["ad"] + 64 * d:R_COLS["ad"] + 64 * (d + 1)]
        z = w0_ref[d:d + 1, :] + _dot(jnp.tanh(wd).astype(BF16), w2_ref[d])
        w_log = -(jnp.maximum(-z, 0.0) + jnp.log(1.0 + jnp.exp(-jnp.abs(z)))) - 0.5
        lw = -jnp.exp(w_log)
        a = _sigmoid(a0_ref[d:d + 1, :] + _dot(ad.astype(BF16), a2_ref[d]))
        kd = k * (1.0 + (a - 1.0) * ka_ref[...])
        ksum = ksum + kd
        b = kk * a
        c = _dot_lexact(tri_ref[d], lw)
        tot = _dot_lexact(sel_ref[...], lw)
        cend = jnp.concatenate(
            [jnp.broadcast_to(tot[i:i + 1, :], (CHUNK, w)) for i in range(nchunk)], axis=0)
        e_c = jnp.exp(c)
        e_nc = jnp.exp(-c)
        e_rem = jnp.exp(cend - c)
        rt_ref[d] = (r * e_c).astype(BF16)
        at_ref[d] = (-kk * jnp.exp(c - lw)).astype(BF16)
        bt_ref[d] = (b * e_nc).astype(BF16)
        kt_ref[d] = (kd * e_nc).astype(BF16)
        bh_ref[d] = (b * e_rem).astype(BF16)
        kh_ref[d] = (kd * e_rem).astype(BF16)
        wdec = jnp.exp(tot)
        for i in range(nchunk):
            wdec_ref[d, 0, i] = wdec[i:i + 1, :]
    bonus_ref[...] = _dot_rexact(r * ksum * rk_ref[...], bd) * v


def _prep_consts():
    tr = R_TILE
    idx = np.arange(tr)
    same = (idx[:, None] // CHUNK) == (idx[None, :] // CHUNK)
    tri = np.stack([same & (idx[:, None] >= idx[None, :]), same & (idx[:, None] <= idx[None, :])])
    sel = (np.arange(8)[:, None] == (idx[None, :] // CHUNK))
    hd = np.arange(RWKV_WIDTH) // HEAD_DIM
    bd = hd[:, None] == hd[None, :]
    as_bf16 = lambda m: jnp.asarray(m.astype(np.float32), dtype=BF16)
    return dict(tri=as_bf16(tri), sel=as_bf16(sel), bd512=as_bf16(bd))


def _rwkv_prep(u, bsz, seq, shift_w, w0, w2, a0, a2, g2, k_k, k_a, r_k, consts):
    tr = R_TILE
    nj = seq // tr
    t = bsz * seq
    w = RWKV_WIDTH
    u3 = u.reshape(bsz, seq, RWKV_COLS)
    rows8 = tr // 8
    cur = pl.BlockSpec((1, tr, RWKV_COLS), lambda b, j: (b, j, 0))
    prev = pl.BlockSpec((1, 8, RWKV_COLS), lambda b, j: (b, jnp.maximum(j * rows8 - 1, 0), 0))
    nxt = pl.BlockSpec((1, 8, RWKV_COLS), lambda b, j: (b, jnp.minimum((j + 1) * rows8, seq // 8 - 1), 0))
    tok2 = pl.BlockSpec((2, tr, w), lambda b, j: (0, b * nj + j, 0))
    tok = pl.BlockSpec((tr, w), lambda b, j: (b * nj + j, 0))
    nchunk = tr // CHUNK
    wdec_spec = pl.BlockSpec((2, 1, nchunk, 1, w), lambda b, j: (0, b, j, 0, 0))
    small = [shift_w, w0, w2, a0, a2, g2, k_k, k_a, r_k, consts["bd512"], consts["tri"], consts["sel"]]
    sd2 = jax.ShapeDtypeStruct((2, t, w), BF16)
    return pl.pallas_call(
        _rwkv_prep_kernel,
        grid=(bsz, nj),
        in_specs=[cur, prev, nxt] + [_full(a.shape) for a in small],
        out_specs=[tok2] * 6 + [tok, wdec_spec, tok, tok],
        out_shape=[sd2] * 6 + [jax.ShapeDtypeStruct((t, w), BF16),
                               jax.ShapeDtypeStruct((2, bsz, seq // CHUNK, 1, w), F32),
                               jax.ShapeDtypeStruct((t, w), F32), jax.ShapeDtypeStruct((t, w), F32)],
        compiler_params=_cparams("parallel", "parallel"),
        name="rwkv_prep",
    )(u3, u3, u3, *small)


S_TILE = 256
LEVELS = (2, 4, 8, 16, 32)
N_DIR_OPERANDS = 6


def _rwkv_scan_kernel(*refs):
    nop = N_DIR_OPERANDS
    dir_refs = refs[:2 * nop]
    vf_ref, vb_ref, wf_ref, wb_ref, mask_ref, yf_ref, yb_ref, state = refs[2 * nop:]
    j = pl.program_id(2)

    @pl.when(j == 0)
    def _():
        state[...] = jnp.zeros_like(state)

    nchunk = S_TILE // CHUNK
    nh = GROUP // HEAD_DIM
    bdm = mask_ref[0]
    eye = mask_ref[1].astype(F32)
    tile4 = lambda x: jnp.concatenate([x] * nh, axis=0)

    for d, (v_ref, w_ref, y_ref) in enumerate(((vf_ref, wf_ref, yf_ref), (vb_ref, wb_ref, yb_ref))):
        rt_ref, at_ref, bt_ref, kt_ref, bh_ref, kh_ref = [dir_refs[2 * i + d] for i in range(nop)]
        strict = mask_ref[2 + d].astype(F32)
        incl = mask_ref[4 + d].astype(F32)
        for ci in range(nchunk):
            c = ci if d == 0 else nchunk - 1 - ci
            rows = slice(c * CHUNK, (c + 1) * CHUNK)
            a_bd = tile4(at_ref[0, rows, :]) * bdm
            r_bd = tile4(rt_ref[0, rows, :]) * bdm
            b_t = tile4(bt_ref[0, rows, :])
            k_t = tile4(kt_ref[0, rows, :])
            b_h = tile4(bh_ref[0, rows, :])
            k_h = tile4(kh_ref[0, rows, :])
            v_bd = tile4(v_ref[0, rows, :]) * bdm
            a_ab = _dot_nt(a_bd, b_t) * strict
            a_ak = (_dot_nt(a_bd, k_t) * strict).astype(BF16)
            a_rb = (_dot_nt(r_bd, b_t) * incl).astype(BF16)
            a_rk = (_dot_nt(r_bd, k_t) * incl).astype(BF16)
            tinv = eye + a_ab * mask_ref[6 + d].astype(F32)
            for li in range(len(LEVELS)):
                lm = (a_ab * mask_ref[8 + 2 * li + d].astype(F32)).astype(BF16)
                tb = tinv.astype(BF16)
                tinv = tinv + _dot(tb, _dot(lm, tb).astype(BF16))
            s_bf = state[d].astype(BF16)
            rhs = _dot_nt(a_bd, s_bf) + _dot(a_ak, v_bd)
            u_bf = _dot(tinv.astype(BF16), rhs.astype(BF16)).astype(BF16)
            y = _dot_nt(r_bd, s_bf) + _dot(a_rb, u_bf) + _dot(a_rk, v_bd)
            y_nat = y[0:CHUNK]
            for hh in range(1, nh):
                y_nat = y_nat + y[hh * CHUNK:(hh + 1) * CHUNK]
            y_ref[0, rows, :] = y_nat
            upd = (_dot_tn(u_bf, b_h) + _dot_tn(v_bd, k_h)) * bdm.astype(F32)
            state[d] = state[d] * w_ref[0, 0, c] + upd


def _scan_masks():
    n = GROUP
    idx = np.arange(n)
    head = idx // CHUNK
    t = idx % CHUNK
    same = head[:, None] == head[None, :]
    tt, ss = t[:, None], t[None, :]
    masks = [same, np.eye(n, dtype=bool)]
    masks += [same & (tt > ss), same & (tt < ss)]
    masks += [same & (tt >= ss), same & (tt <= ss)]
    for m in (1,) + LEVELS:
        blk = same & (tt // (2 * m) == ss // (2 * m))
        lower = blk & ((tt // m) % 2 == 1) & ((ss // m) % 2 == 0)
        upper = blk & ((tt // m) % 2 == 0) & ((ss // m) % 2 == 1)
        masks += [lower, upper]
    return jnp.asarray(np.stack(masks).astype(np.float32), dtype=BF16)


def _rwkv_scan(dir_ops, v, wdec, bsz, seq, masks):
    st = S_TILE
    nj = seq // st
    ng = RWKV_WIDTH // GROUP
    w = RWKV_WIDTH
    t = bsz * seq
    nchunk = st // CHUNK
    blk = (1, st, GROUP)
    fwd = lambda d: pl.BlockSpec(blk, lambda b, g, j: (d, b * nj + j, g))
    bwd = lambda d: pl.BlockSpec(blk, lambda b, g, j: (d, b * nj + (nj - 1 - j), g))
    wf = pl.BlockSpec((1, 1, nchunk, 1, GROUP), lambda b, g, j: (0, b, j, 0, g))
    wb = pl.BlockSpec((1, 1, nchunk, 1, GROUP), lambda b, g, j: (1, b, nj - 1 - j, 0, g))
    v3 = v.reshape(1, t, w)
    in_specs, args = [], []
    for arr in dir_ops:
        in_specs += [fwd(0), bwd(1)]
        args += [arr, arr]
    in_specs += [fwd(0), bwd(0), wf, wb, _full(masks.shape)]
    args += [v3, v3, wdec, wdec, masks]
    sd = jax.ShapeDtypeStruct((1, t, w), F32)
    yf, yb = pl.pallas_call(
        _rwkv_scan_kernel,
        grid=(bsz, ng, nj),
        in_specs=in_specs,
        out_specs=[fwd(0), bwd(0)],
        out_shape=[sd, sd],
        scratch_shapes=[pltpu.VMEM((2, GROUP, GROUP), F32)],
        compiler_params=_cparams("parallel", "parallel", "arbitrary"),
        name="rwkv_scan",
    )(*args)
    return yf.reshape(t, w), yb.reshape(t, w)


def _outproj_kernel(h_ref, o1_ref, o2_ref, o3_ref, l1_ref, l2_ref, l3_ref, attg_ref,
                    yf_ref, yb_ref, bonus_ref, g_ref, lnw_ref, lnb_ref, bd_ref, w_ref, out_ref):
    bd = bd_ref[...]
    inv_n = 1.0 / HEAD_DIM
    l1, l2, l3 = l1_ref[...], l2_ref[...], l3_ref[...]
    m = jnp.maximum(jnp.maximum(l1, l2), l3)
    w1, w2, w3 = jnp.exp(l1 - m), jnp.exp(l2 - m), jnp.exp(l3 - m)
    o = (w1 * o1_ref[...] + w2 * o2_ref[...] + w3 * o3_ref[...]) / (w1 + w2 + w3)
    att = o * lax.rsqrt(_dot_rexact(o * o, bd) * inv_n + NORM_EPS) * attg_ref[...]
    y = yf_ref[...] + yb_ref[...]
    yc = y - _dot_rexact(y, bd) * inv_n
    var = _dot_rexact(yc * yc, bd) * inv_n
    yn = yc * lax.rsqrt(var + LN_X_EPS) * lnw_ref[...] + lnb_ref[...]
    rw = (yn + bonus_ref[...]) * g_ref[...]
    out_ref[...] = (h_ref[...] + _dot(att.astype(BF16), w_ref[:ATT_WIDTH, :])
                    + _dot(rw.astype(BF16), w_ref[ATT_WIDTH:, :]))


def _outproj(h, att_parts, att_g, yf, yb, bonus, g, ln_w, ln_b, bd512, w_out_bf16, tm=256):
    t = h.shape[0]
    row = lambda i: (i, 0)
    half = pl.BlockSpec((tm, ATT_WIDTH), row)
    (o1, l1), (o2, l2), (o3, l3) = att_parts
    vec = _full((1, ATT_WIDTH))
    return pl.pallas_call(
        _outproj_kernel,
        grid=(t // tm,),
        in_specs=[pl.BlockSpec((tm, D_MODEL), row)] + [half] * 6 + [vec] + [half] * 4 + [vec, vec,
                  _full(bd512.shape), _full(w_out_bf16.shape)],
        out_specs=pl.BlockSpec((tm, D_MODEL), row),
        out_shape=jax.ShapeDtypeStruct((t, D_MODEL), F32),
        compiler_params=_cparams("parallel"),
        name="outproj",
    )(h, o1, o2, o3, l1, l2, l3, att_g, yf, yb, bonus, g, ln_w, ln_b, bd512, w_out_bf16)


P_TILE = 256


def _extract_top(cur, nrounds, on_round):
    for r in range(nrounds):
        m = jnp.max(cur, axis=0, keepdims=True)
        on_round(r, m)
        if r + 1 < nrounds:
            cur = jnp.where(cur == m, NEG_INF, cur)


def _peer_route_kernel(h_ref, g_ref, wq_ref, sk_ref, xn_ref, s0_ref, s1_ref, e0_ref, e1_ref, thr_ref, qt):
    tp = P_TILE
    nk = PEER_N_KEYS
    k = PEER_TOPK
    xn = _rms(h_ref[...], g_ref[...]).astype(BF16)
    xn_ref[...] = xn
    qt[...] = _dot_nt(wq_ref[...], xn)
    row16 = lax.broadcasted_iota(jnp.int32, (k, tp), 0)

    def head_body(h, carry):
        tops = []
        scores = []
        for p in range(2):
            hp = 2 * h + p
            q_hp = qt[pl.ds(pl.multiple_of(hp * nk, nk), nk), :].astype(BF16)
            s = _dot(sk_ref[hp], q_hp)
            scores.append(s)
            box = [jnp.full((k, tp), NEG_INF, F32)]

            def on_round(r, m, box=box):
                box[0] = jnp.where(row16 == r, m, box[0])

            _extract_top(s, k, on_round)
            tops.append(box[0])
        t0, t1 = tops
        cand = jnp.concatenate([t0[a:a + 1, :] + t1 for a in range(k)], axis=0)
        top = t0[0:1, :] + t1[0:1, :]
        acc = [jnp.zeros((1, tp), F32), top]

        def on_cand(r, m, acc=acc):
            acc[0] = acc[0] + jnp.exp(m - top)
            acc[1] = m

        _extract_top(cand, k, on_cand)
        z, thr = acc
        s0_ref[h] = scores[0]
        s1_ref[h] = scores[1]
        e0_ref[h] = jnp.exp(scores[0] - t0[0:1, :]) / z
        e1_ref[h] = jnp.exp(scores[1] - t1[0:1, :])
        thr_ref[h] = jnp.broadcast_to(thr, (8, tp))
        return carry

    lax.fori_loop(0, PEER_HEADS, head_body, 0)


def _peer_route(h, g, wq_t_bf16, sub_keys_bf16):
    t = h.shape[0]
    tp = P_TILE
    nk = PEER_N_KEYS
    nh = PEER_HEADS
    row = lambda i: (i, 0)
    lanes = lambda i: (0, 0, i)
    sd = jax.ShapeDtypeStruct((nh, nk, t), F32)
    spec = pl.BlockSpec((nh, nk, tp), lanes)
    return pl.pallas_call(
        _peer_route_kernel,
        grid=(t // tp,),
        in_specs=[pl.BlockSpec((tp, D_MODEL), row), _full((1, D_MODEL)), _full(wq_t_bf16.shape),
                  _full(sub_keys_bf16.shape)],
        out_specs=[pl.BlockSpec((tp, D_MODEL), row), spec, spec, spec, spec, pl.BlockSpec((nh, 8, tp), lanes)],
        out_shape=[jax.ShapeDtypeStruct((t, D_MODEL), BF16), sd, sd, sd, sd,
                   jax.ShapeDtypeStruct((nh, 8, t), F32)],
        scratch_shapes=[pltpu.VMEM((nh * 2 * nk, tp), F32)],
        compiler_params=_cparams("parallel"),
        name="peer_route",
    )(h, g, wq_t_bf16, sub_keys_bf16)


E_TILE = 512


def _peer_dense_kernel(xn_ref, u_ref, vt_ref, s0_ref, s1_ref, e0_ref, e1_ref, thr_ref, h_ref, out_ref, acc):
    e = pl.program_id(1)
    tp = P_TILE
    nk = PEER_N_KEYS

    @pl.when(e == 0)
    def _():
        acc[...] = jnp.zeros_like(acc)

    hid = _dot_nt(u_ref[...], xn_ref[...])
    act = 0.5 * hid * (1.0 + lax.erf(hid * (2.0 ** -0.5)))
    parts = []
    for ii in range(E_TILE // nk):
        i = e * (E_TILE // nk) + ii
        gate = jnp.zeros((nk, tp), F32)
        for h in range(PEER_HEADS):
            s0_row = s0_ref[h, pl.ds(i, 1), :]
            e0_row = e0_ref[h, pl.ds(i, 1), :]
            chosen = (s1_ref[h] + s0_row) >= thr_ref[h, 0:1, :]
            gate = gate + jnp.where(chosen, e1_ref[h] * e0_row, 0.0)
        parts.append((act[ii * nk:(ii + 1) * nk, :] * gate).astype(BF16))
    acc[...] += _dot(vt_ref[...], jnp.concatenate(parts, axis=0))

    @pl.when(e == pl.num_programs(1) - 1)
    def _():
        out_ref[...] = h_ref[...] + acc[...].T


def _peer_dense(h, xn, u_bf16, vt_bf16, s0, s1, e0, e1, thr):
    t = h.shape[0]
    tp = P_TILE
    ne = u_bf16.shape[0]
    nh, nk = PEER_HEADS, PEER_N_KEYS
    tok = lambda i, e: (i, 0)
    lanes = lambda i, e: (0, 0, i)
    spec = pl.BlockSpec((nh, nk, tp), lanes)
    return pl.pallas_call(
        _peer_dense_kernel,
        grid=(t // tp, ne // E_TILE),
        in_specs=[pl.BlockSpec((tp, D_MODEL), tok), pl.BlockSpec((E_TILE, D_MODEL), lambda i, e: (e, 0)),
                  pl.BlockSpec((D_MODEL, E_TILE), lambda i, e: (0, e)), spec, spec, spec, spec,
                  pl.BlockSpec((nh, 8, tp), lanes), pl.BlockSpec((tp, D_MODEL), tok)],
        out_specs=pl.BlockSpec((tp, D_MODEL), tok),
        out_shape=jax.ShapeDtypeStruct((t, D_MODEL), F32),
        scratch_shapes=[pltpu.VMEM((D_MODEL, tp), F32)],
        compiler_params=_cparams("parallel", "arbitrary"),
        name="peer_dense",
    )(xn, u_bf16, vt_bf16, s0, s1, e0, e1, thr, h)


def _ple_kernel(h_ref, p_ref, g_ref, wg_ref, b_ref, wp_ref, fg_ref, out_ref, *, final):
    h = h_ref[...]
    gate = _sigmoid(_dot(_rms(h, g_ref[...]).astype(BF16), wg_ref[...]) + b_ref[...])
    out = h + gate * _dot(p_ref[...].astype(BF16), wp_ref[...])
    out_ref[...] = _rms(out, fg_ref[...]) if final else out


def _ple(h, p, g, wg_bf16, b, wp_bf16, final_g, final, tm=512):
    t = h.shape[0]
    row = lambda i: (i, 0)
    vec = _full((1, D_MODEL))
    return pl.pallas_call(
        functools.partial(_ple_kernel, final=final),
        grid=(t // tm,),
        in_specs=[pl.BlockSpec((tm, D_MODEL), row), pl.BlockSpec((tm, PLE_DIM), row), vec,
                  _full(wg_bf16.shape), vec, _full(wp_bf16.shape), vec],
        out_specs=pl.BlockSpec((tm, D_MODEL), row),
        out_shape=jax.ShapeDtypeStruct((t, D_MODEL), F32),
        compiler_params=_cparams("parallel"),
        name="ple_final" if final else "ple",
    )(h, p, g, wg_bf16, b, wp_bf16, final_g)


def _mixer(h, bsz, seq, lw, consts):
    q, k, v, u = _inproj(h, lw["norm_mix_g"], lw["w_in"], consts["cos"], consts["sin"], seq)
    att_parts = [_attn_pattern(q, k, v, bsz, seq, dil) for dil in DILATIONS]
    prep = _rwkv_prep(u, bsz, seq, lw["rwkv_shift"], lw["rwkv_w0"], lw["rwkv_w2"], lw["rwkv_a0"],
                      lw["rwkv_a2"], lw["rwkv_g2"], lw["rwkv_k_k"], lw["rwkv_k_a"], lw["rwkv_r_k"], consts)
    dir_ops, (rv, wdec, bonus, g) = prep[:N_DIR_OPERANDS], prep[N_DIR_OPERANDS:]
    yf, yb = _rwkv_scan(dir_ops, rv, wdec, bsz, seq, consts["scan_masks"])
    return _outproj(h, att_parts, lw["att_norm_g"], yf, yb, bonus, g, lw["rwkv_ln_w"], lw["rwkv_ln_b"],
                    consts["bd512"], lw["w_out"])


def _peer(h, lw):
    xn, s0, s1, e0, e1, thr = _peer_route(h, lw["norm_ffn_g"], lw["peer_w_q_t"], lw["peer_sub_keys"])
    return _peer_dense(h, xn, lw["peer_u"], lw["peer_v_t"], s0, s1, e0, e1, thr)


def _layer_weights(i, wts):
    row = lambda a: a[i].reshape(1, -1)
    bf = lambda a: a[i].astype(BF16)
    return dict(
        norm_mix_g=row(wts["norm_mix_g"]), w_in=bf(wts["w_in"]), att_norm_g=row(wts["att_norm_g"]),
        rwkv_shift=wts["rwkv_shift"][i], rwkv_w0=wts["rwkv_w0"][i], rwkv_w2=bf(wts["rwkv_w2"]),
        rwkv_a0=wts["rwkv_a0"][i], rwkv_a2=bf(wts["rwkv_a2"]), rwkv_g2=bf(wts["rwkv_g2"]),
        rwkv_k_k=row(wts["rwkv_k_k"]), rwkv_k_a=row(wts["rwkv_k_a"]), rwkv_r_k=row(wts["rwkv_r_k"]),
        rwkv_ln_w=row(wts["rwkv_ln_w"]), rwkv_ln_b=row(wts["rwkv_ln_b"]), w_out=bf(wts["w_out"]),
        norm_ffn_g=row(wts["norm_ffn_g"]), peer_w_q_t=bf(wts["peer_w_q"]).T,
        peer_sub_keys=bf(wts["peer_sub_keys"]).reshape(2 * PEER_HEADS, PEER_N_KEYS, PEER_N_KEYS),
        peer_u=bf(wts["peer_u"]), peer_v_t=bf(wts["peer_v"]).T,
        ple_norm_g=row(wts["ple_norm_g"]), ple_w_gate=bf(wts["ple_w_gate"]), ple_b_gate=row(wts["ple_b_gate"]),
        ple_w_proj=bf(wts["ple_w_proj"]),
    )


def _trunk(x, p, wts, final_g):
    bsz, seq, _ = x.shape
    depth = p.shape[0]
    t = bsz * seq
    consts = _prep_consts()
    consts["cos"], consts["sin"] = _rope_tables(seq)
    consts["scan_masks"] = _scan_masks()
    h = x.reshape(t, D_MODEL)
    fg = final_g.reshape(1, D_MODEL)
    for i in range(depth):
        lw = _layer_weights(i, wts)
        h = _mixer(h, bsz, seq, lw, consts)
        h = _peer(h, lw)
        h = _ple(h, p[i].reshape(t, PLE_DIM), lw["ple_norm_g"], lw["ple_w_gate"], lw["ple_b_gate"],
                 lw["ple_w_proj"], fg, final=(i == depth - 1))
    return h.reshape(bsz, seq, D_MODEL)


def kernel(x_prompt, x_sample, p_prompt, p_sample, norm_mix_g, w_in, att_norm_g, rwkv_shift, rwkv_w0, rwkv_w2, rwkv_a0, rwkv_a2, rwkv_g2, rwkv_k_k, rwkv_k_a, rwkv_r_k, rwkv_ln_w, rwkv_ln_b, w_out, norm_ffn_g, peer_w_q, peer_sub_keys, peer_u, peer_v, ple_norm_g, ple_w_gate, ple_b_gate, ple_w_proj, final_norm_g):
    wts = dict(norm_mix_g=norm_mix_g, w_in=w_in, att_norm_g=att_norm_g, rwkv_shift=rwkv_shift, rwkv_w0=rwkv_w0,
               rwkv_w2=rwkv_w2, rwkv_a0=rwkv_a0, rwkv_a2=rwkv_a2, rwkv_g2=rwkv_g2, rwkv_k_k=rwkv_k_k,
               rwkv_k_a=rwkv_k_a, rwkv_r_k=rwkv_r_k, rwkv_ln_w=rwkv_ln_w, rwkv_ln_b=rwkv_ln_b, w_out=w_out,
               norm_ffn_g=norm_ffn_g, peer_w_q=peer_w_q, peer_sub_keys=peer_sub_keys, peer_u=peer_u,
               peer_v=peer_v, ple_norm_g=ple_norm_g, ple_w_gate=ple_w_gate, ple_b_gate=ple_b_gate,
               ple_w_proj=ple_w_proj)
    nb = x_prompt.shape[0]
    x = jnp.concatenate([x_prompt, x_sample], axis=0)
    p = jnp.concatenate([p_prompt, p_sample], axis=1)
    y = _trunk(x, p, wts, final_norm_g)
    return (y[:nb], y[nb:])
```

```python
import functools

import jax
import jax.numpy as jnp
import numpy as np
from jax import lax
from jax.experimental import pallas as pl
from jax.experimental.pallas import tpu as pltpu

F32 = jnp.float32
BF16 = jnp.bfloat16

D_MODEL = 1024
HEAD_DIM = 64
ATT_WIDTH = 512
ATT_HEADS = 8
RWKV_WIDTH = 512
RWKV_COLS = 1920
QKV_COLS = 3 * ATT_WIDTH
DILATIONS = (1, 4, 16)
HALF_WINDOW = 64
ROPE_THETA = 500000.0
ROPE_DIM = 16
NORM_EPS = 1e-6
LN_X_EPS = 64e-5
NEG_INF = -1e30
PEER_HEADS = 8
PEER_N_KEYS = 128
PEER_TOPK = 16
PLE_DIM = 256
CHUNK = 64
GROUP = 256

VMEM_LIMIT = 56 * 1024 * 1024


def _cparams(*sem):
    return pltpu.CompilerParams(dimension_semantics=sem, vmem_limit_bytes=VMEM_LIMIT)


def _full(shape):
    nd = len(shape)
    return pl.BlockSpec(shape, lambda *_: (0,) * nd)


def _dot(a, b):
    return jnp.dot(a, b, preferred_element_type=F32)


def _dot_nt(a, b):
    return lax.dot_general(a, b, (((1,), (1,)), ((), ())), preferred_element_type=F32)


def _dot_tn(a, b):
    return lax.dot_general(a, b, (((0,), (0,)), ((), ())), preferred_element_type=F32)


def _split3(x):
    hi = x.astype(BF16)
    r1 = x - hi.astype(F32)
    mid = r1.astype(BF16)
    lo = (r1 - mid.astype(F32)).astype(BF16)
    return hi, mid, lo


def _dot_lexact(l_bf16, x):
    hi, mid, lo = _split3(x)
    return _dot(l_bf16, hi) + _dot(l_bf16, mid) + _dot(l_bf16, lo)


def _dot_rexact(x, r_bf16):
    hi, mid, lo = _split3(x)
    return _dot(hi, r_bf16) + _dot(mid, r_bf16) + _dot(lo, r_bf16)


def _rms(x, g):
    return x * lax.rsqrt(jnp.mean(x * x, axis=-1, keepdims=True) + NORM_EPS) * g


def _sigmoid(x):
    return 1.0 / (1.0 + jnp.exp(-x))


def _inproj_kernel(h_ref, g_ref, w_ref, cos_ref, sin_ref, q_ref, k_ref, v_ref, u_ref):
    xn = _rms(h_ref[...], g_ref[...]).astype(BF16)
    qkv = _dot(xn, w_ref[:, :QKV_COLS])
    u_ref[...] = _dot(xn, w_ref[:, QKV_COLS:])
    cos = cos_ref[...]
    sin = sin_ref[...]
    lane = lax.broadcasted_iota(jnp.int32, cos.shape, 1) % HEAD_DIM
    half = ROPE_DIM // 2

    def rot(t):
        partner = jnp.where(lane < half, pltpu.roll(t, ATT_WIDTH - half, 1), pltpu.roll(t, half, 1))
        return t * cos + partner * sin

    q_ref[...] = (rot(qkv[:, :ATT_WIDTH]) * (HEAD_DIM ** -0.5)).astype(BF16)
    k_ref[...] = rot(qkv[:, ATT_WIDTH:2 * ATT_WIDTH]).astype(BF16)
    v_ref[...] = qkv[:, 2 * ATT_WIDTH:].astype(BF16)


def _inproj(h, g, w_bf16, cos_t, sin_t, seq, tm=512):
    t = h.shape[0]
    tm = min(tm, seq)
    nseq = seq // tm
    row = lambda i: (i, 0)
    pos = lambda i: (i % nseq, 0)
    return pl.pallas_call(
        _inproj_kernel,
        grid=(t // tm,),
        in_specs=[pl.BlockSpec((tm, D_MODEL), row), _full((1, D_MODEL)), _full(w_bf16.shape),
                  pl.BlockSpec((tm, ATT_WIDTH), pos), pl.BlockSpec((tm, ATT_WIDTH), pos)],
        out_specs=[pl.BlockSpec((tm, ATT_WIDTH), row)] * 3 + [pl.BlockSpec((tm, RWKV_COLS), row)],
        out_shape=[jax.ShapeDtypeStruct((t, ATT_WIDTH), BF16)] * 3 + [jax.ShapeDtypeStruct((t, RWKV_COLS), F32)],
        compiler_params=_cparams("parallel"),
        name="inproj",
    )(h, g, w_bf16, cos_t, sin_t)


def _rope_tables(seq):
    half = ROPE_DIM // 2
    inv_freq = ROPE_THETA ** (-jnp.arange(half, dtype=F32) / half)
    ang = jnp.arange(seq, dtype=F32)[:, None] * inv_freq[None, :]
    ones = jnp.ones((seq, HEAD_DIM - ROPE_DIM), F32)
    cos_h = jnp.concatenate([jnp.cos(ang), jnp.cos(ang), ones], axis=1)
    sin_h = jnp.concatenate([-jnp.sin(ang), jnp.sin(ang), 0.0 * ones], axis=1)
    return jnp.tile(cos_h, (1, ATT_HEADS)), jnp.tile(sin_h, (1, ATT_HEADS))


ATT_SUB = 128


def _attn_kernel(q_ref, kp_ref, kc_ref, kn_ref, vp_ref, vc_ref, vn_ref, o_ref, lse_ref,
                 kbuf, vbuf, *, length, tq):
    j = pl.program_id(2)
    hw = HALF_WINDOW
    kbuf[0:hw] = kp_ref[0]
    kbuf[hw:hw + tq] = kc_ref[0]
    kbuf[hw + tq:] = kn_ref[0]
    vbuf[0:hw] = vp_ref[0]
    vbuf[hw:hw + tq] = vc_ref[0]
    vbuf[hw + tq:] = vn_ref[0]
    nk = ATT_SUB + 2 * hw
    qi = lax.broadcasted_iota(jnp.int32, (ATT_SUB, nk), 0)
    ki = lax.broadcasted_iota(jnp.int32, (ATT_SUB, nk), 1)
    band = jnp.abs(ki - hw - qi) <= hw
    for s in range(tq // ATT_SUB):
        q0 = s * ATT_SUB
        kpos = j * tq + (q0 - hw) + ki
        valid = band & (kpos >= 0) & (kpos < length)
        for h in range(ATT_HEADS):
            cols = slice(h * HEAD_DIM, (h + 1) * HEAD_DIM)
            sc = _dot_nt(q_ref[0, q0:q0 + ATT_SUB, cols], kbuf[q0:q0 + nk, cols])
            sc = jnp.where(valid, sc, NEG_INF)
            m = jnp.max(sc, axis=-1, keepdims=True)
            p = jnp.exp(sc - m)
            l = jnp.sum(p, axis=-1, keepdims=True)
            o = _dot(p.astype(BF16), vbuf[q0:q0 + nk, cols])
            o_ref[0, q0:q0 + ATT_SUB, cols] = o / l
            lse_ref[0, q0:q0 + ATT_SUB, cols] = jnp.broadcast_to(m + jnp.log(l), (ATT_SUB, HEAD_DIM))


def _attn_pattern(q, k, v, bsz, seq, dil):
    length = seq // dil
    tq = min(length, 512)
    nq = length // tq
    nh = length // HALF_WINDOW
    rq = tq // HALF_WINDOW
    view = lambda a: a.reshape(bsz, length, dil * ATT_WIDTH)
    cur = pl.BlockSpec((1, tq, ATT_WIDTH), lambda b, r, j: (b, j, r))
    prev = pl.BlockSpec((1, HALF_WINDOW, ATT_WIDTH), lambda b, r, j: (b, jnp.maximum(j * rq - 1, 0), r))
    nxt = pl.BlockSpec((1, HALF_WINDOW, ATT_WIDTH), lambda b, r, j: (b, jnp.minimum((j + 1) * rq, nh - 1), r))
    out_sd = jax.ShapeDtypeStruct((bsz, length, dil * ATT_WIDTH), F32)
    o, lse = pl.pallas_call(
        functools.partial(_attn_kernel, length=length, tq=tq),
        grid=(bsz, dil, nq),
        in_specs=[cur, prev, cur, nxt, prev, cur, nxt],
        out_specs=[cur, cur],
        out_shape=[out_sd, out_sd],
        scratch_shapes=[pltpu.VMEM((tq + 2 * HALF_WINDOW, ATT_WIDTH), BF16)] * 2,
        compiler_params=_cparams("parallel", "parallel", "parallel"),
        name=f"attn_d{dil}",
    )(view(q), view(k), view(k), view(k), view(v), view(v), view(v))
    t = bsz * seq
    return o.reshape(t, ATT_WIDTH), lse.reshape(t, ATT_WIDTH)


R_TILE = 256
R_COLS = dict(r=0, k=512, v=1024, wd=1536, ad=1664, gd=1792)


def _rwkv_prep_kernel(uc_ref, up_ref, un_ref, shift_ref, w0_ref, w2_ref, a0_ref, a2_ref, g2_ref,
                      kk_ref, ka_ref, rk_ref, bd_ref, tri_ref, sel_ref,
                      rt_ref, at_ref, bt_ref, kt_ref, bh_ref, kh_ref, v_ref, wdec_ref, bonus_ref, g_ref):
    j = pl.program_id(1)
    nj = pl.num_programs(1)
    tr = R_TILE
    cur = uc_ref[0]
    row = lax.broadcasted_iota(jnp.int32, cur.shape, 0)
    prev_edge = up_ref[0, 7:8, :] * (j > 0).astype(F32)
    next_edge = un_ref[0, 0:1, :] * (j < nj - 1).astype(F32)
    prev = jnp.where(row == 0, prev_edge, pltpu.roll(cur, 1, 0))
    nxt = jnp.where(row == tr - 1, next_edge, pltpu.roll(cur, tr - 1, 0))
    us = shift_ref[0:1, :] * prev + shift_ref[1:2, :] * cur + shift_ref[2:3, :] * nxt

    w = RWKV_WIDTH
    r = us[:, R_COLS["r"]:R_COLS["r"] + w]
    k = us[:, R_COLS["k"]:R_COLS["k"] + w]
    v = us[:, R_COLS["v"]:R_COLS["v"] + w]
    gd = us[:, R_COLS["gd"]:]
    bd = bd_ref[...]

    g_ref[...] = _dot(_sigmoid(gd).astype(BF16), g2_ref[...])
    v_ref[...] = v.astype(BF16)
    kk = k * kk_ref[...]
    kk = kk / jnp.maximum(jnp.sqrt(_dot_rexact(kk * kk, bd)), 1e-12)

    nchunk = tr // CHUNK
    ksum = jnp.zeros_like(k)
    for d in range(2):
        wd = us[:, R_COLS["wd"] + 64 * d:R_COLS["wd"] + 64 * (d + 1)]
        ad = us[:, R_COLS["ad"] + 64 * d:R_COLS["ad"] + 64 * (d + 1)]
        z = w0_ref[d:d + 1, :] + _dot(jnp.tanh(wd).astype(BF16), w2_ref[d])
        w_log = -(jnp.maximum(-z, 0.0) + jnp.log(1.0 + jnp.exp(-jnp.abs(z)))) - 0.5
        lw = -jnp.exp(w_log)
        a = _sigmoid(a0_ref[d:d + 1, :] + _dot(ad.astype(BF16), a2_ref[d]))
        kd = k * (1.0 + (a - 1.0) * ka_ref[...])
        ksum = ksum + kd
        b = kk * a
        c = _dot_lexact(tri_ref[d], lw)
        tot = _dot_lexact(sel_ref[...], lw)
        cend = jnp.concatenate(
            [jnp.broadcast_to(tot[i:i + 1, :], (CHUNK, w)) for i in range(nchunk)], axis=0)
        e_c = jnp.exp(c)
        e_nc = jnp.exp(-c)
        e_rem = jnp.exp(cend - c)
        rt_ref[d] = (r * e_c).astype(BF16)
        at_ref[d] = (-kk * jnp.exp(c - lw)).astype(BF16)
        bt_ref[d] = (b * e_nc).astype(BF16)
        kt_ref[d] = (kd * e_nc).astype(BF16)
        bh_ref[d] = (b * e_rem).astype(BF16)
        kh_ref[d] = (kd * e_rem).astype(BF16)
        wdec = jnp.exp(tot)
        for i in range(nchunk):
            wdec_ref[d, 0, i] = wdec[i:i + 1, :]
    bonus_ref[...] = _dot_rexact(r * ksum * rk_ref[...], bd) * v


def _prep_consts():
    tr = R_TILE
    idx = np.arange(tr)
    same = (idx[:, None] // CHUNK) == (idx[None, :] // CHUNK)
    tri = np.stack([same & (idx[:, None] >= idx[None, :]), same & (idx[:, None] <= idx[None, :])])
    sel = (np.arange(8)[:, None] == (idx[None, :] // CHUNK))
    hd = np.arange(RWKV_WIDTH) // HEAD_DIM
    bd = hd[:, None] == hd[None, :]
    as_bf16 = lambda m: jnp.asarray(m.astype(np.float32), dtype=BF16)
    return dict(tri=as_bf16(tri), sel=as_bf16(sel), bd512=as_bf16(bd))


def _rwkv_prep(u, bsz, seq, shift_w, w0, w2, a0, a2, g2, k_k, k_a, r_k, consts):
    tr = R_TILE
    nj = seq // tr
    t = bsz * seq
    w = RWKV_WIDTH
    u3 = u.reshape(bsz, seq, RWKV_COLS)
    rows8 = tr // 8
    cur = pl.BlockSpec((1, tr, RWKV_COLS), lambda b, j: (b, j, 0))
    prev = pl.BlockSpec((1, 8, RWKV_COLS), lambda b, j: (b, jnp.maximum(j * rows8 - 1, 0), 0))
    nxt = pl.BlockSpec((1, 8, RWKV_COLS), lambda b, j: (b, jnp.minimum((j + 1) * rows8, seq // 8 - 1), 0))
    tok2 = pl.BlockSpec((2, tr, w), lambda b, j: (0, b * nj + j, 0))
    tok = pl.BlockSpec((tr, w), lambda b, j: (b * nj + j, 0))
    nchunk = tr // CHUNK
    wdec_spec = pl.BlockSpec((2, 1, nchunk, 1, w), lambda b, j: (0, b, j, 0, 0))
    small = [shift_w, w0, w2, a0, a2, g2, k_k, k_a, r_k, consts["bd512"], consts["tri"], consts["sel"]]
    sd2 = jax.ShapeDtypeStruct((2, t, w), BF16)
    return pl.pallas_call(
        _rwkv_prep_kernel,
        grid=(bsz, nj),
        in_specs=[cur, prev, nxt] + [_full(a.shape) for a in small],
        out_specs=[tok2] * 6 + [tok, wdec_spec, tok, tok],
        out_shape=[sd2] * 6 + [jax.ShapeDtypeStruct((t, w), BF16),
                               jax.ShapeDtypeStruct((2, bsz, seq // CHUNK, 1, w), F32),
                               jax.ShapeDtypeStruct((t, w), F32), jax.ShapeDtypeStruct((t, w), F32)],
        compiler_params=_cparams("parallel", "parallel"),
        name="rwkv_prep",
    )(u3, u3, u3, *small)


S_TILE = 256
LEVELS = (2, 4, 8, 16, 32)
N_DIR_OPERANDS = 6


def _rwkv_scan_kernel(*refs):
    nop = N_DIR_OPERANDS
    dir_refs = refs[:2 * nop]
    vf_ref, vb_ref, wf_ref, wb_ref, mask_ref, yf_ref, yb_ref, state = refs[2 * nop:]
    j = pl.program_id(2)

    @pl.when(j == 0)
    def _():
        state[...] = jnp.zeros_like(state)

    nchunk = S_TILE // CHUNK
    nh = GROUP // HEAD_DIM
    bdm = mask_ref[0]
    bdm_f32 = bdm.astype(F32)
    eye = mask_ref[1].astype(F32)
    tile4 = lambda x: jnp.concatenate([x] * nh, axis=0)
    io = ((vf_ref, wf_ref, yf_ref), (vb_ref, wb_ref, yb_ref))
    order = [(d, ci if d == 0 else nchunk - 1 - ci) for ci in range(nchunk) for d in range(2)]

    pre = {}
    for d, c in order:
        rt_ref, at_ref, bt_ref, kt_ref, bh_ref, kh_ref = [dir_refs[2 * i + d] for i in range(nop)]
        rows = slice(c * CHUNK, (c + 1) * CHUNK)
        a_bd = tile4(at_ref[0, rows, :]) * bdm
        r_bd = tile4(rt_ref[0, rows, :]) * bdm
        v_bd = tile4(io[d][0][0, rows, :]) * bdm
        ar = jnp.concatenate([a_bd, r_bd], axis=0)
        ar_b = _dot_nt(ar, tile4(bt_ref[0, rows, :]))
        ar_k = _dot_nt(ar, tile4(kt_ref[0, rows, :]))
        strict = mask_ref[2 + d].astype(F32)
        incl = mask_ref[4 + d].astype(F32)
        a_ab = ar_b[:GROUP] * strict
        a_ak = (ar_k[:GROUP] * strict).astype(BF16)
        a_r = jnp.concatenate([(ar_b[GROUP:] * incl).astype(BF16), (ar_k[GROUP:] * incl).astype(BF16)], axis=1)
        hat = jnp.concatenate([tile4(bh_ref[0, rows, :]), tile4(kh_ref[0, rows, :])], axis=0)
        pre[d, c] = dict(a_bd=a_bd, r_bd=r_bd, v_bd=v_bd, a_ab=a_ab, a_ak=a_ak, a_r=a_r, hat=hat,
                         tinv=eye + a_ab * mask_ref[6 + d].astype(F32))
    for li in range(len(LEVELS)):
        for d, c in order:
            p = pre[d, c]
            lm = (p["a_ab"] * mask_ref[8 + 2 * li + d].astype(F32)).astype(BF16)
            tb = p["tinv"].astype(BF16)
            p["tinv"] = p["tinv"] + _dot(tb, _dot(lm, tb).astype(BF16))
    s = [state[0], state[1]]
    for d, c in order:
        p = pre[d, c]
        w_ref, y_ref = io[d][1], io[d][2]
        rows = slice(c * CHUNK, (c + 1) * CHUNK)
        s_bf = s[d].astype(BF16)
        rhs = _dot_nt(p["a_bd"], s_bf) + _dot(p["a_ak"], p["v_bd"])
        u_bf = _dot(p["tinv"].astype(BF16), rhs.astype(BF16)).astype(BF16)
        uv = jnp.concatenate([u_bf, p["v_bd"]], axis=0)
        y = _dot_nt(p["r_bd"], s_bf) + _dot(p["a_r"], uv)
        y_nat = y[0:CHUNK]
        for hh in range(1, nh):
            y_nat = y_nat + y[hh * CHUNK:(hh + 1) * CHUNK]
        y_ref[0, rows, :] = y_nat
        s[d] = s[d] * w_ref[0, 0, c] + _dot_tn(uv, p["hat"]) * bdm_f32
    state[0] = s[0]
    state[1] = s[1]


def _scan_masks():
    n = GROUP
    idx = np.arange(n)
    head = idx // CHUNK
    t = idx % CHUNK
    same = head[:, None] == head[None, :]
    tt, ss = t[:, None], t[None, :]
    masks = [same, np.eye(n, dtype=bool)]
    masks += [same & (tt > ss), same & (tt < ss)]
    masks += [same & (tt >= ss), same & (tt <= ss)]
    for m in (1,) + LEVELS:
        blk = same & (tt // (2 * m) == ss // (2 * m))
        lower = blk & ((tt // m) % 2 == 1) & ((ss // m) % 2 == 0)
        upper = blk & ((tt // m) % 2 == 0) & ((ss // m) % 2 == 1)
        masks += [lower, upper]
    return jnp.asarray(np.stack(masks).astype(np.float32), dtype=BF16)


def _rwkv_scan(dir_ops, v, wdec, bsz, seq, masks):
    st = S_TILE
    nj = seq // st
    ng = RWKV_WIDTH // GROUP
    w = RWKV_WIDTH
    t = bsz * seq
    nchunk = st // CHUNK
    blk = (1, st, GROUP)
    fwd = lambda d: pl.BlockSpec(blk, lambda b, g, j: (d, b * nj + j, g))
    bwd = lambda d: pl.BlockSpec(blk, lambda b, g, j: (d, b * nj + (nj - 1 - j), g))
    wf = pl.BlockSpec((1, 1, nchunk, 1, GROUP), lambda b, g, j: (0, b, j, 0, g))
    wb = pl.BlockSpec((1, 1, nchunk, 1, GROUP), lambda b, g, j: (1, b, nj - 1 - j, 0, g))
    v3 = v.reshape(1, t, w)
    in_specs, args = [], []
    for arr in dir_ops:
        in_specs += [fwd(0), bwd(1)]
        args += [arr, arr]
    in_specs += [fwd(0), bwd(0), wf, wb, _full(masks.shape)]
    args += [v3, v3, wdec, wdec, masks]
    sd = jax.ShapeDtypeStruct((1, t, w), F32)
    yf, yb = pl.pallas_call(
        _rwkv_scan_kernel,
        grid=(bsz, ng, nj),
        in_specs=in_specs,
        out_specs=[fwd(0), bwd(0)],
        out_shape=[sd, sd],
        scratch_shapes=[pltpu.VMEM((2, GROUP, GROUP), F32)],
        compiler_params=_cparams("parallel", "parallel", "arbitrary"),
        name="rwkv_scan",
    )(*args)
    return yf.reshape(t, w), yb.reshape(t, w)


def _outproj_kernel(h_ref, o1_ref, o2_ref, o3_ref, l1_ref, l2_ref, l3_ref, attg_ref,
                    yf_ref, yb_ref, bonus_ref, g_ref, lnw_ref, lnb_ref, bd_ref, w_ref, out_ref):
    bd = bd_ref[...]
    inv_n = 1.0 / HEAD_DIM
    l1, l2, l3 = l1_ref[...], l2_ref[...], l3_ref[...]
    m = jnp.maximum(jnp.maximum(l1, l2), l3)
    w1, w2, w3 = jnp.exp(l1 - m), jnp.exp(l2 - m), jnp.exp(l3 - m)
    o = (w1 * o1_ref[...] + w2 * o2_ref[...] + w3 * o3_ref[...]) / (w1 + w2 + w3)
    att = o * lax.rsqrt(_dot_rexact(o * o, bd) * inv_n + NORM_EPS) * attg_ref[...]
    y = yf_ref[...] + yb_ref[...]
    yc = y - _dot_rexact(y, bd) * inv_n
    var = _dot_rexact(yc * yc, bd) * inv_n
    yn = yc * lax.rsqrt(var + LN_X_EPS) * lnw_ref[...] + lnb_ref[...]
    rw = (yn + bonus_ref[...]) * g_ref[...]
    out_ref[...] = (h_ref[...] + _dot(att.astype(BF16), w_ref[:ATT_WIDTH, :])
                    + _dot(rw.astype(BF16), w_ref[ATT_WIDTH:, :]))


def _outproj(h, att_parts, att_g, yf, yb, bonus, g, ln_w, ln_b, bd512, w_out_bf16, tm=256):
    t = h.shape[0]
    row = lambda i: (i, 0)
    half = pl.BlockSpec((tm, ATT_WIDTH), row)
    (o1, l1), (o2, l2), (o3, l3) = att_parts
    vec = _full((1, ATT_WIDTH))
    return pl.pallas_call(
        _outproj_kernel,
        grid=(t // tm,),
        in_specs=[pl.BlockSpec((tm, D_MODEL), row)] + [half] * 6 + [vec] + [half] * 4 + [vec, vec,
                  _full(bd512.shape), _full(w_out_bf16.shape)],
        out_specs=pl.BlockSpec((tm, D_MODEL), row),
        out_shape=jax.ShapeDtypeStruct((t, D_MODEL), F32),
        compiler_params=_cparams("parallel"),
        name="outproj",
    )(h, o1, o2, o3, l1, l2, l3, att_g, yf, yb, bonus, g, ln_w, ln_b, bd512, w_out_bf16)


ROUTE_TILE = 256


def _top_rounds(cur, nrounds, on_round, track_rank):
    rank = jnp.full(cur.shape, float(nrounds), F32) if track_rank else None
    for r in range(nrounds):
        m = jnp.max(cur, axis=0, keepdims=True)
        on_round(r, m)
        hit = cur == m
        if track_rank:
            rank = jnp.where(hit, float(r), rank)
        if r + 1 < nrounds:
            cur = jnp.where(hit, NEG_INF, cur)
    return rank


def _peer_route_kernel(h_ref, g_ref, wq_ref, sk_ref, xn_ref, rank1_ref, e1_ref, cnt_ref, e0_ref, qt):
    tp = ROUTE_TILE
    nk = PEER_N_KEYS
    k = PEER_TOPK
    xn = _rms(h_ref[...], g_ref[...]).astype(BF16)
    xn_ref[...] = xn
    qt[...] = _dot_nt(wq_ref[...], xn)
    row16 = lax.broadcasted_iota(jnp.int32, (k, tp), 0)

    def head_body(h, carry):
        tops, scores, ranks = [], [], []
        for p in range(2):
            hp = 2 * h + p
            q_hp = qt[pl.ds(pl.multiple_of(hp * nk, nk), nk), :].astype(BF16)
            s = _dot(sk_ref[hp], q_hp)
            scores.append(s)
            box = [jnp.full((k, tp), NEG_INF, F32)]

            def on_round(r, m, box=box):
                box[0] = jnp.where(row16 == r, m, box[0])

            ranks.append(_top_rounds(s, k, on_round, track_rank=(p == 1)))
            tops.append(box[0])
        t0, t1 = tops
        half = k // 2
        cand = jnp.concatenate([t0[0:1, :] + t1] + [t0[a:a + 1, :] + t1[0:half, :] for a in range(1, half)]
                               + [t0[half:, :] + t1[0:1, :]], axis=0)
        top = t0[0:1, :] + t1[0:1, :]
        acc = [jnp.zeros((1, tp), F32), top]

        def on_cand(r, m, acc=acc):
            acc[0] = acc[0] + jnp.exp(m - top)
            acc[1] = m

        _top_rounds(cand, k, on_cand, track_rank=False)
        z, thr = acc
        n = jnp.zeros((k, tp), F32)
        for b in range(k):
            n = n + jnp.where(t0 + t1[b:b + 1, :] >= thr, 1.0, 0.0)
        cnt = jnp.zeros((nk, tp), F32)
        for a in range(k):
            cnt = jnp.where(scores[0] == t0[a:a + 1, :], n[a:a + 1, :], cnt)
        rank1_ref[h] = ranks[1].astype(BF16)
        e1_ref[h] = jnp.exp(scores[1] - t1[0:1, :]).astype(BF16)
        cnt_ref[h] = cnt
        e0_ref[h] = jnp.exp(scores[0] - t0[0:1, :]) / z
        return carry

    lax.fori_loop(0, PEER_HEADS, head_body, 0)


def _peer_route(h, g, wq_t_bf16, sub_keys_bf16):
    t = h.shape[0]
    tp = ROUTE_TILE
    nk = PEER_N_KEYS
    nh = PEER_HEADS
    row = lambda i: (i, 0)
    lanes = lambda i: (0, 0, i)
    spec = pl.BlockSpec((nh, nk, tp), lanes)
    sd = lambda dt: jax.ShapeDtypeStruct((nh, nk, t), dt)
    return pl.pallas_call(
        _peer_route_kernel,
        grid=(t // tp,),
        in_specs=[pl.BlockSpec((tp, D_MODEL), row), _full((1, D_MODEL)), _full(wq_t_bf16.shape),
                  _full(sub_keys_bf16.shape)],
        out_specs=[pl.BlockSpec((tp, D_MODEL), row), spec, spec, spec, spec],
        out_shape=[jax.ShapeDtypeStruct((t, D_MODEL), BF16), sd(BF16), sd(BF16), sd(F32), sd(F32)],
        scratch_shapes=[pltpu.VMEM((nh * 2 * nk, tp), F32)],
        compiler_params=_cparams("parallel"),
        name="peer_route",
    )(h, g, wq_t_bf16, sub_keys_bf16)


E_TILE = 1024
DENSE_TILE = 512
E_SUB = 512


def _peer_dense_kernel(xn_ref, u_ref, vt_ref, rank1_ref, e1_ref, cnt_ref, e0_ref, h_ref, out_ref, acc):
    e = pl.program_id(1)
    nk = PEER_N_KEYS
    tp = xn_ref.shape[0]

    @pl.when(e == 0)
    def _():
        acc[...] = jnp.zeros_like(acc)

    xn = xn_ref[...]
    total = None
    for sub in range(E_TILE // E_SUB):
        rows = slice(sub * E_SUB, (sub + 1) * E_SUB)
        hid = _dot_nt(u_ref[rows, :], xn)
        act = (0.5 * hid * (1.0 + lax.erf(hid * (2.0 ** -0.5)))).astype(BF16)
        parts = []
        for ii in range(E_SUB // nk):
            i = e * (E_TILE // nk) + sub * (E_SUB // nk) + ii
            gate = jnp.zeros((nk, tp), BF16)
            for h in range(PEER_HEADS):
                cnt_row = jnp.broadcast_to(cnt_ref[h, pl.ds(i, 1), :], (nk, tp)).astype(BF16)
                e0_row = jnp.broadcast_to(e0_ref[h, pl.ds(i, 1), :], (nk, tp)).astype(BF16)
                gate = gate + jnp.where(rank1_ref[h] < cnt_row, e1_ref[h] * e0_row, jnp.zeros((), BF16))
            parts.append(act[ii * nk:(ii + 1) * nk, :] * gate)
        part = _dot(vt_ref[:, rows], jnp.concatenate(parts, axis=0))
        total = part if total is None else total + part
    acc[...] += total

    @pl.when(e == pl.num_programs(1) - 1)
    def _():
        out_ref[...] = h_ref[...] + acc[...].T


def _peer_dense(h, xn, u_bf16, vt_bf16, rank1, e1, cnt, e0):
    t = h.shape[0]
    tp = min(DENSE_TILE, t)
    ne = u_bf16.shape[0]
    nh, nk = PEER_HEADS, PEER_N_KEYS
    tok = lambda i, e: (i, 0)
    lanes = lambda i, e: (0, 0, i)
    spec = pl.BlockSpec((nh, nk, tp), lanes)
    return pl.pallas_call(
        _peer_dense_kernel,
        grid=(t // tp, ne // E_TILE),
        in_specs=[pl.BlockSpec((tp, D_MODEL), tok), pl.BlockSpec((E_TILE, D_MODEL), lambda i, e: (e, 0)),
                  pl.BlockSpec((D_MODEL, E_TILE), lambda i, e: (0, e)), spec, spec, spec, spec,
                  pl.BlockSpec((tp, D_MODEL), tok)],
        out_specs=pl.BlockSpec((tp, D_MODEL), tok),
        out_shape=jax.ShapeDtypeStruct((t, D_MODEL), F32),
        scratch_shapes=[pltpu.VMEM((D_MODEL, tp), F32)],
        compiler_params=_cparams("parallel", "arbitrary"),
        name="peer_dense",
    )(xn, u_bf16, vt_bf16, rank1, e1, cnt, e0, h)


def _ple_kernel(h_ref, p_ref, g_ref, wg_ref, b_ref, wp_ref, fg_ref, out_ref, *, final):
    h = h_ref[...]
    gate = _sigmoid(_dot(_rms(h, g_ref[...]).astype(BF16), wg_ref[...]) + b_ref[...])
    out = h + gate * _dot(p_ref[...].astype(BF16), wp_ref[...])
    out_ref[...] = _rms(out, fg_ref[...]) if final else out


def _ple(h, p, g, wg_bf16, b, wp_bf16, final_g, final, tm=512):
    t = h.shape[0]
    row = lambda i: (i, 0)
    vec = _full((1, D_MODEL))
    return pl.pallas_call(
        functools.partial(_ple_kernel, final=final),
        grid=(t // tm,),
        in_specs=[pl.BlockSpec((tm, D_MODEL), row), pl.BlockSpec((tm, PLE_DIM), row), vec,
                  _full(wg_bf16.shape), vec, _full(wp_bf16.shape), vec],
        out_specs=pl.BlockSpec((tm, D_MODEL), row),
        out_shape=jax.ShapeDtypeStruct((t, D_MODEL), F32),
        compiler_params=_cparams("parallel"),
        name="ple_final" if final else "ple",
    )(h, p, g, wg_bf16, b, wp_bf16, final_g)


def _mixer(h, bsz, seq, lw, consts):
    q, k, v, u = _inproj(h, lw["norm_mix_g"], lw["w_in"], consts["cos"], consts["sin"], seq)
    att_parts = [_attn_pattern(q, k, v, bsz, seq, dil) for dil in DILATIONS]
    prep = _rwkv_prep(u, bsz, seq, lw["rwkv_shift"], lw["rwkv_w0"], lw["rwkv_w2"], lw["rwkv_a0"],
                      lw["rwkv_a2"], lw["rwkv_g2"], lw["rwkv_k_k"], lw["rwkv_k_a"], lw["rwkv_r_k"], consts)
    dir_ops, (rv, wdec, bonus, g) = prep[:N_DIR_OPERANDS], prep[N_DIR_OPERANDS:]
    yf, yb = _rwkv_scan(dir_ops, rv, wdec, bsz, seq, consts["scan_masks"])
    return _outproj(h, att_parts, lw["att_norm_g"], yf, yb, bonus, g, lw["rwkv_ln_w"], lw["rwkv_ln_b"],
                    consts["bd512"], lw["w_out"])


def _peer(h, lw):
    xn, rank1, e1, cnt, e0 = _peer_route(h, lw["norm_ffn_g"], lw["peer_w_q_t"], lw["peer_sub_keys"])
    return _peer_dense(h, xn, lw["peer_u"], lw["peer_v_t"], rank1, e1, cnt, e0)


def _layer_weights(i, wts):
    row = lambda a: a[i].reshape(1, -1)
    bf = lambda a: a[i].astype(BF16)
    return dict(
        norm_mix_g=row(wts["norm_mix_g"]), w_in=bf(wts["w_in"]), att_norm_g=row(wts["att_norm_g"]),
        rwkv_shift=wts["rwkv_shift"][i], rwkv_w0=wts["rwkv_w0"][i], rwkv_w2=bf(wts["rwkv_w2"]),
        rwkv_a0=wts["rwkv_a0"][i], rwkv_a2=bf(wts["rwkv_a2"]), rwkv_g2=bf(wts["rwkv_g2"]),
        rwkv_k_k=row(wts["rwkv_k_k"]), rwkv_k_a=row(wts["rwkv_k_a"]), rwkv_r_k=row(wts["rwkv_r_k"]),
        rwkv_ln_w=row(wts["rwkv_ln_w"]), rwkv_ln_b=row(wts["rwkv_ln_b"]), w_out=bf(wts["w_out"]),
        norm_ffn_g=row(wts["norm_ffn_g"]), peer_w_q_t=bf(wts["peer_w_q"]).T,
        peer_sub_keys=bf(wts["peer_sub_keys"]).reshape(2 * PEER_HEADS, PEER_N_KEYS, PEER_N_KEYS),
        peer_u=bf(wts["peer_u"]), peer_v_t=bf(wts["peer_v"]).T,
        ple_norm_g=row(wts["ple_norm_g"]), ple_w_gate=bf(wts["ple_w_gate"]), ple_b_gate=row(wts["ple_b_gate"]),
        ple_w_proj=bf(wts["ple_w_proj"]),
    )


def _trunk(x, p, wts, final_g):
    bsz, seq, _ = x.shape
    depth = p.shape[0]
    t = bsz * seq
    consts = _prep_consts()
    consts["cos"], consts["sin"] = _rope_tables(seq)
    consts["scan_masks"] = _scan_masks()
    h = x.reshape(t, D_MODEL)
    fg = final_g.reshape(1, D_MODEL)
    for i in range(depth):
        lw = _layer_weights(i, wts)
        h = _mixer(h, bsz, seq, lw, consts)
        h = _peer(h, lw)
        h = _ple(h, p[i].reshape(t, PLE_DIM), lw["ple_norm_g"], lw["ple_w_gate"], lw["ple_b_gate"],
                 lw["ple_w_proj"], fg, final=(i == depth - 1))
    return h.reshape(bsz, seq, D_MODEL)


def kernel(x_prompt, x_sample, p_prompt, p_sample, norm_mix_g, w_in, att_norm_g, rwkv_shift, rwkv_w0, rwkv_w2, rwkv_a0, rwkv_a2, rwkv_g2, rwkv_k_k, rwkv_k_a, rwkv_r_k, rwkv_ln_w, rwkv_ln_b, w_out, norm_ffn_g, peer_w_q, peer_sub_keys, peer_u, peer_v, ple_norm_g, ple_w_gate, ple_b_gate, ple_w_proj, final_norm_g):
    wts = dict(norm_mix_g=norm_mix_g, w_in=w_in, att_norm_g=att_norm_g, rwkv_shift=rwkv_shift, rwkv_w0=rwkv_w0,
               rwkv_w2=rwkv_w2, rwkv_a0=rwkv_a0, rwkv_a2=rwkv_a2, rwkv_g2=rwkv_g2, rwkv_k_k=rwkv_k_k,
               rwkv_k_a=rwkv_k_a, rwkv_r_k=rwkv_r_k, rwkv_ln_w=rwkv_ln_w, rwkv_ln_b=rwkv_ln_b, w_out=w_out,
               norm_ffn_g=norm_ffn_g, peer_w_q=peer_w_q, peer_sub_keys=peer_sub_keys, peer_u=peer_u,
               peer_v=peer_v, ple_norm_g=ple_norm_g, ple_w_gate=ple_w_gate, ple_b_gate=ple_b_gate,
               ple_w_proj=ple_w_proj)
    nb = x_prompt.shape[0]
    x = jnp.concatenate([x_prompt, x_sample], axis=0)
    p = jnp.concatenate([p_prompt, p_sample], axis=1)
    y = _trunk(x, p, wts, final_norm_g)
    return (y[:nb], y[nb:])
```

```python
import functools

import jax
import jax.numpy as jnp
import numpy as np
from jax import lax
from jax.experimental import pallas as pl
from jax.experimental.pallas import tpu as pltpu

F32 = jnp.float32
BF16 = jnp.bfloat16

D_MODEL = 1024
HEAD_DIM = 64
ATT_WIDTH = 512
ATT_HEADS = 8
RWKV_WIDTH = 512
RWKV_COLS = 1920
QKV_COLS = 3 * ATT_WIDTH
DILATIONS = (1, 4, 16)
HALF_WINDOW = 64
ROPE_THETA = 500000.0
ROPE_DIM = 16
NORM_EPS = 1e-6
LN_X_EPS = 64e-5
NEG_INF = -1e30
PEER_HEADS = 8
PEER_N_KEYS = 128
PEER_TOPK = 16
PLE_DIM = 256
CHUNK = 64
GROUP = 256

VMEM_LIMIT = 56 * 1024 * 1024


def _cparams(*sem, flags=None):
    return pltpu.CompilerParams(dimension_semantics=sem, vmem_limit_bytes=VMEM_LIMIT, flags=flags)


def _full(shape):
    nd = len(shape)
    return pl.BlockSpec(shape, lambda *_: (0,) * nd)


def _dot(a, b):
    return jnp.dot(a, b, preferred_element_type=F32)


def _dot_nt(a, b):
    return lax.dot_general(a, b, (((1,), (1,)), ((), ())), preferred_element_type=F32)


def _dot_tn(a, b):
    return lax.dot_general(a, b, (((0,), (0,)), ((), ())), preferred_element_type=F32)


def _split3(x):
    hi = x.astype(BF16)
    r1 = x - hi.astype(F32)
    mid = r1.astype(BF16)
    lo = (r1 - mid.astype(F32)).astype(BF16)
    return hi, mid, lo


def _dot_lexact(l_bf16, x):
    hi, mid, lo = _split3(x)
    return _dot(l_bf16, hi) + _dot(l_bf16, mid) + _dot(l_bf16, lo)


def _dot_rexact(x, r_bf16):
    hi, mid, lo = _split3(x)
    return _dot(hi, r_bf16) + _dot(mid, r_bf16) + _dot(lo, r_bf16)


def _rms(x, g):
    return x * lax.rsqrt(jnp.mean(x * x, axis=-1, keepdims=True) + NORM_EPS) * g


def _sigmoid(x):
    return 1.0 / (1.0 + jnp.exp(-x))


LANES = 128
ATT_LANE_TILES = ATT_WIDTH // LANES


def _to_classes(t, stage, out_refs):
    rows = t.shape[0]
    for c in range(ATT_LANE_TILES):
        stage[c] = t[:, c * LANES:(c + 1) * LANES]
    for dil, ref in zip(DILATIONS, out_refs):
        if dil == 1:
            ref[...] = t.astype(ref.dtype)
            continue
        for r in range(dil):
            for c in range(ATT_LANE_TILES):
                lo = r * ATT_WIDTH + c * LANES
                ref[:, lo:lo + LANES] = stage[c, pl.ds(r, rows // dil, stride=dil), :].astype(ref.dtype)


def _from_classes(ref, dil, stage):
    if dil == 1:
        return ref[...]
    rows = ref.shape[0] * dil
    for r in range(dil):
        for c in range(ATT_LANE_TILES):
            lo = r * ATT_WIDTH + c * LANES
            stage[c, pl.ds(r, rows // dil, stride=dil), :] = ref[:, lo:lo + LANES]
    return jnp.concatenate([stage[c] for c in range(ATT_LANE_TILES)], axis=1)


def _inproj_kernel(h_ref, g_ref, w_ref, cos_ref, sin_ref, *refs):
    nd = len(DILATIONS)
    q_refs, k_refs, v_refs = refs[:nd], refs[nd:2 * nd], refs[2 * nd:3 * nd]
    u_ref, stage = refs[3 * nd:]
    xn = _rms(h_ref[...], g_ref[...]).astype(BF16)
    qkv = _dot(xn, w_ref[:, :QKV_COLS])
    u_ref[...] = _dot(xn, w_ref[:, QKV_COLS:])
    cos = cos_ref[...]
    sin = sin_ref[...]
    lane = lax.broadcasted_iota(jnp.int32, cos.shape, 1) % HEAD_DIM
    half = ROPE_DIM // 2

    def rot(t):
        partner = jnp.where(lane < half, pltpu.roll(t, ATT_WIDTH - half, 1), pltpu.roll(t, half, 1))
        return t * cos + partner * sin

    _to_classes(rot(qkv[:, :ATT_WIDTH]) * (HEAD_DIM ** -0.5), stage, q_refs)
    _to_classes(rot(qkv[:, ATT_WIDTH:2 * ATT_WIDTH]), stage, k_refs)
    _to_classes(qkv[:, 2 * ATT_WIDTH:], stage, v_refs)


def _inproj(h, g, w_bf16, cos_t, sin_t, seq, tm=512):
    t = h.shape[0]
    tm = min(tm, seq)
    nseq = seq // tm
    row = lambda i: (i, 0)
    pos = lambda i: (i % nseq, 0)
    cls_specs = [pl.BlockSpec((tm // d, d * ATT_WIDTH), row) for d in DILATIONS]
    cls_shapes = [jax.ShapeDtypeStruct((t // d, d * ATT_WIDTH), BF16) for d in DILATIONS]
    outs = pl.pallas_call(
        _inproj_kernel,
        grid=(t // tm,),
        in_specs=[pl.BlockSpec((tm, D_MODEL), row), _full((1, D_MODEL)), _full(w_bf16.shape),
                  pl.BlockSpec((tm, ATT_WIDTH), pos), pl.BlockSpec((tm, ATT_WIDTH), pos)],
        out_specs=cls_specs * 3 + [pl.BlockSpec((tm, RWKV_COLS), row)],
        out_shape=cls_shapes * 3 + [jax.ShapeDtypeStruct((t, RWKV_COLS), F32)],
        scratch_shapes=[pltpu.VMEM((ATT_LANE_TILES, tm, LANES), F32)],
        compiler_params=_cparams("parallel"),
        name="inproj",
    )(h, g, w_bf16, cos_t, sin_t)
    nd = len(DILATIONS)
    return outs[:nd], outs[nd:2 * nd], outs[2 * nd:3 * nd], outs[3 * nd]


def _rope_tables(seq):
    half = ROPE_DIM // 2
    inv_freq = ROPE_THETA ** (-jnp.arange(half, dtype=F32) / half)
    ang = jnp.arange(seq, dtype=F32)[:, None] * inv_freq[None, :]
    ones = jnp.ones((seq, HEAD_DIM - ROPE_DIM), F32)
    cos_h = jnp.concatenate([jnp.cos(ang), jnp.cos(ang), ones], axis=1)
    sin_h = jnp.concatenate([-jnp.sin(ang), jnp.sin(ang), 0.0 * ones], axis=1)
    return jnp.tile(cos_h, (1, ATT_HEADS)), jnp.tile(sin_h, (1, ATT_HEADS))


ATT_SUB = 128


def _attn_kernel(q_ref, kp_ref, kc_ref, kn_ref, vp_ref, vc_ref, vn_ref, o_ref, lse_ref,
                 kbuf, vbuf, *, length, tq):
    j = pl.program_id(2)
    hw = HALF_WINDOW
    kbuf[0:hw] = kp_ref[0]
    kbuf[hw:hw + tq] = kc_ref[0]
    kbuf[hw + tq:] = kn_ref[0]
    vbuf[0:hw] = vp_ref[0]
    vbuf[hw:hw + tq] = vc_ref[0]
    vbuf[hw + tq:] = vn_ref[0]
    nk = ATT_SUB + 2 * hw
    qi = lax.broadcasted_iota(jnp.int32, (ATT_SUB, nk), 0)
    ki = lax.broadcasted_iota(jnp.int32, (ATT_SUB, nk), 1)
    band = jnp.abs(ki - hw - qi) <= hw
    for s in range(tq // ATT_SUB):
        q0 = s * ATT_SUB
        kpos = j * tq + (q0 - hw) + ki
        valid = band & (kpos >= 0) & (kpos < length)
        heads = [slice(h * HEAD_DIM, (h + 1) * HEAD_DIM) for h in range(ATT_HEADS)]
        sc = jnp.concatenate([_dot_nt(q_ref[0, q0:q0 + ATT_SUB, c], kbuf[q0:q0 + nk, c]) for c in heads], axis=0)
        sc = jnp.where(jnp.concatenate([valid] * ATT_HEADS, axis=0), sc, NEG_INF)
        m = jnp.max(sc, axis=-1, keepdims=True)
        p = jnp.exp(sc - m)
        l = jnp.sum(p, axis=-1, keepdims=True)
        pb = p.astype(BF16)
        inv_l = 1.0 / l
        lse = m + jnp.log(l)
        for h, c in enumerate(heads):
            rows = slice(h * ATT_SUB, (h + 1) * ATT_SUB)
            o_ref[0, q0:q0 + ATT_SUB, c] = _dot(pb[rows], vbuf[q0:q0 + nk, c]) * inv_l[rows]
            lse_ref[0, q0:q0 + ATT_SUB, c] = jnp.broadcast_to(lse[rows], (ATT_SUB, HEAD_DIM))


def _attn_pattern(q, k, v, bsz, seq, dil):
    length = seq // dil
    tq = min(length, 512)
    nq = length // tq
    nh = length // HALF_WINDOW
    rq = tq // HALF_WINDOW
    view = lambda a: a.reshape(bsz, length, dil * ATT_WIDTH)
    cur = pl.BlockSpec((1, tq, ATT_WIDTH), lambda b, r, j: (b, j, r))
    prev = pl.BlockSpec((1, HALF_WINDOW, ATT_WIDTH), lambda b, r, j: (b, jnp.maximum(j * rq - 1, 0), r))
    nxt = pl.BlockSpec((1, HALF_WINDOW, ATT_WIDTH), lambda b, r, j: (b, jnp.minimum((j + 1) * rq, nh - 1), r))
    out_sd = jax.ShapeDtypeStruct((bsz, length, dil * ATT_WIDTH), F32)
    o, lse = pl.pallas_call(
        functools.partial(_attn_kernel, length=length, tq=tq),
        grid=(bsz, dil, nq),
        in_specs=[cur, prev, cur, nxt, prev, cur, nxt],
        out_specs=[cur, cur],
        out_shape=[out_sd, out_sd],
        scratch_shapes=[pltpu.VMEM((tq + 2 * HALF_WINDOW, ATT_WIDTH), BF16)] * 2,
        compiler_params=_cparams("parallel", "parallel", "parallel"),
        name=f"attn_d{dil}",
    )(view(q), view(k), view(k), view(k), view(v), view(v), view(v))
    flat = (bsz * length, dil * ATT_WIDTH)
    return o.reshape(flat), lse.reshape(flat)


R_TILE = 256
R_COLS = dict(r=0, k=512, v=1024, wd=1536, ad=1664, gd=1792)


def _rwkv_prep_kernel(uc_ref, up_ref, un_ref, shift_ref, w0_ref, w2_ref, a0_ref, a2_ref, g2_ref,
                      kk_ref, ka_ref, rk_ref, bd_ref, tri_ref, sel_ref,
                      rt_ref, at_ref, bt_ref, kt_ref, bh_ref, kh_ref, v_ref, wdec_ref, bonus_ref, g_ref):
    j = pl.program_id(1)
    nj = pl.num_programs(1)
    tr = R_TILE
    cur = uc_ref[0]
    row = lax.broadcasted_iota(jnp.int32, cur.shape, 0)
    prev_edge = up_ref[0, 7:8, :] * (j > 0).astype(F32)
    next_edge = un_ref[0, 0:1, :] * (j < nj - 1).astype(F32)
    prev = jnp.where(row == 0, prev_edge, pltpu.roll(cur, 1, 0))
    nxt = jnp.where(row == tr - 1, next_edge, pltpu.roll(cur, tr - 1, 0))
    us = shift_ref[0:1, :] * prev + shift_ref[1:2, :] * cur + shift_ref[2:3, :] * nxt

    w = RWKV_WIDTH
    r = us[:, R_COLS["r"]:R_COLS["r"] + w]
    k = us[:, R_COLS["k"]:R_COLS["k"] + w]
    v = us[:, R_COLS["v"]:R_COLS["v"] + w]
    gd = us[:, R_COLS["gd"]:]
    bd = bd_ref[...]

    g_ref[...] = _dot(_sigmoid(gd).astype(BF16), g2_ref[...])
    v_ref[...] = v.astype(BF16)
    kk = k * kk_ref[...]
    kk = kk / jnp.maximum(jnp.sqrt(_dot_rexact(kk * kk, bd)), 1e-12)

    nchunk = tr // CHUNK
    ksum = jnp.zeros_like(k)
    for d in range(2):
        wd = us[:, R_COLS["wd"] + 64 * d:R_COLS["wd"] + 64 * (d + 1)]
        ad = us[:, R_COLS["ad"] + 64 * d:R_COLS["ad"] + 64 * (d + 1)]
        z = w0_ref[d:d + 1, :] + _dot(jnp.tanh(wd).astype(BF16), w2_ref[d])
        w_log = -(jnp.maximum(-z, 0.0) + jnp.log(1.0 + jnp.exp(-jnp.abs(z)))) - 0.5
        lw = -jnp.exp(w_log)
        a = _sigmoid(a0_ref[d:d + 1, :] + _dot(ad.astype(BF16), a2_ref[d]))
        kd = k * (1.0 + (a - 1.0) * ka_ref[...])
        ksum = ksum + kd
        b = kk * a
        c = _dot_lexact(tri_ref[d], lw)
        tot = _dot_lexact(sel_ref[...], lw)
        cend = jnp.concatenate(
            [jnp.broadcast_to(tot[i:i + 1, :], (CHUNK, w)) for i in range(nchunk)], axis=0)
        e_c = jnp.exp(c)
        e_nc = jnp.exp(-c)
        e_rem = jnp.exp(cend - c)
        rt_ref[d] = (r * e_c).astype(BF16)
        at_ref[d] = (-kk * jnp.exp(c - lw)).astype(BF16)
        bt_ref[d] = (b * e_nc).astype(BF16)
        kt_ref[d] = (kd * e_nc).astype(BF16)
        bh_ref[d] = (b * e_rem).astype(BF16)
        kh_ref[d] = (kd * e_rem).astype(BF16)
        wdec = jnp.exp(tot)
        for i in range(nchunk):
            wdec_ref[d, 0, i] = wdec[i:i + 1, :]
    bonus_ref[...] = _dot_rexact(r * ksum * rk_ref[...], bd) * v


def _prep_consts():
    tr = R_TILE
    idx = np.arange(tr)
    same = (idx[:, None] // CHUNK) == (idx[None, :] // CHUNK)
    tri = np.stack([same & (idx[:, None] >= idx[None, :]), same & (idx[:, None] <= idx[None, :])])
    sel = (np.arange(8)[:, None] == (idx[None, :] // CHUNK))
    hd = np.arange(RWKV_WIDTH) // HEAD_DIM
    bd = hd[:, None] == hd[None, :]
    as_bf16 = lambda m: jnp.asarray(m.astype(np.float32), dtype=BF16)
    return dict(tri=as_bf16(tri), sel=as_bf16(sel), bd512=as_bf16(bd))


def _rwkv_prep(u, bsz, seq, shift_w, w0, w2, a0, a2, g2, k_k, k_a, r_k, consts):
    tr = R_TILE
    nj = seq // tr
    t = bsz * seq
    w = RWKV_WIDTH
    u3 = u.reshape(bsz, seq, RWKV_COLS)
    rows8 = tr // 8
    cur = pl.BlockSpec((1, tr, RWKV_COLS), lambda b, j: (b, j, 0))
    prev = pl.BlockSpec((1, 8, RWKV_COLS), lambda b, j: (b, jnp.maximum(j * rows8 - 1, 0), 0))
    nxt = pl.BlockSpec((1, 8, RWKV_COLS), lambda b, j: (b, jnp.minimum((j + 1) * rows8, seq // 8 - 1), 0))
    tok2 = pl.BlockSpec((2, tr, w), lambda b, j: (0, b * nj + j, 0))
    tok = pl.BlockSpec((tr, w), lambda b, j: (b * nj + j, 0))
    nchunk = tr // CHUNK
    wdec_spec = pl.BlockSpec((2, 1, nchunk, 1, w), lambda b, j: (0, b, j, 0, 0))
    small = [shift_w, w0, w2, a0, a2, g2, k_k, k_a, r_k, consts["bd512"], consts["tri"], consts["sel"]]
    sd2 = jax.ShapeDtypeStruct((2, t, w), BF16)
    return pl.pallas_call(
        _rwkv_prep_kernel,
        grid=(bsz, nj),
        in_specs=[cur, prev, nxt] + [_full(a.shape) for a in small],
        out_specs=[tok2] * 6 + [tok, wdec_spec, tok, tok],
        out_shape=[sd2] * 6 + [jax.ShapeDtypeStruct((t, w), BF16),
                               jax.ShapeDtypeStruct((2, bsz, seq // CHUNK, 1, w), F32),
                               jax.ShapeDtypeStruct((t, w), F32), jax.ShapeDtypeStruct((t, w), F32)],
        compiler_params=_cparams("parallel", "parallel"),
        name="rwkv_prep",
    )(u3, u3, u3, *small)


S_TILE = 256
LEVELS = (2, 4, 8, 16, 32)
N_DIR_OPERANDS = 6


def _rwkv_scan_kernel(*refs):
    nop = N_DIR_OPERANDS
    dir_refs = refs[:2 * nop]
    vf_ref, vb_ref, wf_ref, wb_ref, mask_ref, yf_ref, yb_ref, state = refs[2 * nop:]
    j = pl.program_id(2)

    @pl.when(j == 0)
    def _():
        state[...] = jnp.zeros_like(state)

    nchunk = S_TILE // CHUNK
    nh = GROUP // HEAD_DIM
    bdm = mask_ref[0]
    bdm_f32 = bdm.astype(F32)
    eye = mask_ref[1].astype(F32)
    tile4 = lambda x: jnp.concatenate([x] * nh, axis=0)
    io = ((vf_ref, wf_ref, yf_ref), (vb_ref, wb_ref, yb_ref))
    order = [(d, ci if d == 0 else nchunk - 1 - ci) for ci in range(nchunk) for d in range(2)]

    pre = {}
    for d, c in order:
        rt_ref, at_ref, bt_ref, kt_ref, bh_ref, kh_ref = [dir_refs[2 * i + d] for i in range(nop)]
        rows = slice(c * CHUNK, (c + 1) * CHUNK)
        a_bd = tile4(at_ref[0, rows, :]) * bdm
        r_bd = tile4(rt_ref[0, rows, :]) * bdm
        v_bd = tile4(io[d][0][0, rows, :]) * bdm
        ar = jnp.concatenate([a_bd, r_bd], axis=0)
        ar_b = _dot_nt(ar, tile4(bt_ref[0, rows, :]))
        ar_k = _dot_nt(ar, tile4(kt_ref[0, rows, :]))
        strict = mask_ref[2 + d].astype(F32)
        incl = mask_ref[4 + d].astype(F32)
        a_ab = ar_b[:GROUP] * strict
        a_ak = (ar_k[:GROUP] * strict).astype(BF16)
        a_r = jnp.concatenate([(ar_b[GROUP:] * incl).astype(BF16), (ar_k[GROUP:] * incl).astype(BF16)], axis=1)
        hat = jnp.concatenate([tile4(bh_ref[0, rows, :]), tile4(kh_ref[0, rows, :])], axis=0)
        pre[d, c] = dict(a_bd=a_bd, r_bd=r_bd, v_bd=v_bd, a_ab=a_ab, a_ak=a_ak, a_r=a_r, hat=hat,
                         tinv=eye + a_ab * mask_ref[6 + d].astype(F32))
    for li in range(len(LEVELS)):
        for d, c in order:
            p = pre[d, c]
            lm = (p["a_ab"] * mask_ref[8 + 2 * li + d].astype(F32)).astype(BF16)
            tb = p["tinv"].astype(BF16)
            p["tinv"] = p["tinv"] + _dot(tb, _dot(lm, tb).astype(BF16))
    s = [state[0], state[1]]
    for d, c in order:
        p = pre[d, c]
        w_ref, y_ref = io[d][1], io[d][2]
        rows = slice(c * CHUNK, (c + 1) * CHUNK)
        s_bf = s[d].astype(BF16)
        rhs = _dot_nt(p["a_bd"], s_bf) + _dot(p["a_ak"], p["v_bd"])
        u_bf = _dot(p["tinv"].astype(BF16), rhs.astype(BF16)).astype(BF16)
        uv = jnp.concatenate([u_bf, p["v_bd"]], axis=0)
        y = _dot_nt(p["r_bd"], s_bf) + _dot(p["a_r"], uv)
        y_nat = y[0:CHUNK]
        for hh in range(1, nh):
            y_nat = y_nat + y[hh * CHUNK:(hh + 1) * CHUNK]
        y_ref[0, rows, :] = y_nat
        s[d] = s[d] * w_ref[0, 0, c] + _dot_tn(uv, p["hat"]) * bdm_f32
    state[0] = s[0]
    state[1] = s[1]


def _scan_masks():
    n = GROUP
    idx = np.arange(n)
    head = idx // CHUNK
    t = idx % CHUNK
    same = head[:, None] == head[None, :]
    tt, ss = t[:, None], t[None, :]
    masks = [same, np.eye(n, dtype=bool)]
    masks += [same & (tt > ss), same & (tt < ss)]
    masks += [same & (tt >= ss), same & (tt <= ss)]
    for m in (1,) + LEVELS:
        blk = same & (tt // (2 * m) == ss // (2 * m))
        lower = blk & ((tt // m) % 2 == 1) & ((ss // m) % 2 == 0)
        upper = blk & ((tt // m) % 2 == 0) & ((ss // m) % 2 == 1)
        masks += [lower, upper]
    return jnp.asarray(np.stack(masks).astype(np.float32), dtype=BF16)


def _rwkv_scan(dir_ops, v, wdec, bsz, seq, masks):
    st = S_TILE
    nj = seq // st
    ng = RWKV_WIDTH // GROUP
    w = RWKV_WIDTH
    t = bsz * seq
    nchunk = st // CHUNK
    blk = (1, st, GROUP)
    fwd = lambda d: pl.BlockSpec(blk, lambda b, g, j: (d, b * nj + j, g))
    bwd = lambda d: pl.BlockSpec(blk, lambda b, g, j: (d, b * nj + (nj - 1 - j), g))
    wf = pl.BlockSpec((1, 1, nchunk, 1, GROUP), lambda b, g, j: (0, b, j, 0, g))
    wb = pl.BlockSpec((1, 1, nchunk, 1, GROUP), lambda b, g, j: (1, b, nj - 1 - j, 0, g))
    v3 = v.reshape(1, t, w)
    in_specs, args = [], []
    for arr in dir_ops:
        in_specs += [fwd(0), bwd(1)]
        args += [arr, arr]
    in_specs += [fwd(0), bwd(0), wf, wb, _full(masks.shape)]
    args += [v3, v3, wdec, wdec, masks]
    sd = jax.ShapeDtypeStruct((1, t, w), F32)
    yf, yb = pl.pallas_call(
        _rwkv_scan_kernel,
        grid=(bsz, ng, nj),
        in_specs=in_specs,
        out_specs=[fwd(0), bwd(0)],
        out_shape=[sd, sd],
        scratch_shapes=[pltpu.VMEM((2, GROUP, GROUP), F32)],
        compiler_params=_cparams("parallel", "parallel", "arbitrary"),
        name="rwkv_scan",
    )(*args)
    return yf.reshape(t, w), yb.reshape(t, w)


def _outproj_kernel(h_ref, o1_ref, o2_ref, o3_ref, l1_ref, l2_ref, l3_ref, attg_ref,
                    yf_ref, yb_ref, bonus_ref, g_ref, lnw_ref, lnb_ref, bd_ref, w_ref, out_ref, stage):
    bd = bd_ref[...]
    inv_n = 1.0 / HEAD_DIM
    nd = len(DILATIONS)
    o1, o2, o3 = [_from_classes(r, d, stage.at[i])
                  for i, (r, d) in enumerate(zip((o1_ref, o2_ref, o3_ref), DILATIONS))]
    l1, l2, l3 = [_from_classes(r, d, stage.at[nd + i])
                  for i, (r, d) in enumerate(zip((l1_ref, l2_ref, l3_ref), DILATIONS))]
    m = jnp.maximum(jnp.maximum(l1, l2), l3)
    w1, w2, w3 = jnp.exp(l1 - m), jnp.exp(l2 - m), jnp.exp(l3 - m)
    o = (w1 * o1 + w2 * o2 + w3 * o3) / (w1 + w2 + w3)
    att = o * lax.rsqrt(_dot_rexact(o * o, bd) * inv_n + NORM_EPS) * attg_ref[...]
    y = yf_ref[...] + yb_ref[...]
    yc = y - _dot_rexact(y, bd) * inv_n
    var = _dot_rexact(yc * yc, bd) * inv_n
    yn = yc * lax.rsqrt(var + LN_X_EPS) * lnw_ref[...] + lnb_ref[...]
    rw = (yn + bonus_ref[...]) * g_ref[...]
    out_ref[...] = (h_ref[...] + _dot(att.astype(BF16), w_ref[:ATT_WIDTH, :])
                    + _dot(rw.astype(BF16), w_ref[ATT_WIDTH:, :]))


def _outproj(h, att_parts, att_g, yf, yb, bonus, g, ln_w, ln_b, bd512, w_out_bf16, tm=256):
    t = h.shape[0]
    row = lambda i: (i, 0)
    half = pl.BlockSpec((tm, ATT_WIDTH), row)
    cls = [pl.BlockSpec((tm // d, d * ATT_WIDTH), row) for d in DILATIONS]
    (o1, l1), (o2, l2), (o3, l3) = att_parts
    vec = _full((1, ATT_WIDTH))
    return pl.pallas_call(
        _outproj_kernel,
        grid=(t // tm,),
        in_specs=[pl.BlockSpec((tm, D_MODEL), row)] + cls * 2 + [vec] + [half] * 4 + [vec, vec,
                  _full(bd512.shape), _full(w_out_bf16.shape)],
        out_specs=pl.BlockSpec((tm, D_MODEL), row),
        out_shape=jax.ShapeDtypeStruct((t, D_MODEL), F32),
        scratch_shapes=[pltpu.VMEM((2 * len(DILATIONS), ATT_LANE_TILES, tm, LANES), F32)],
        compiler_params=_cparams("parallel"),
        name="outproj",
    )(h, o1, o2, o3, l1, l2, l3, att_g, yf, yb, bonus, g, ln_w, ln_b, bd512, w_out_bf16)


ROUTE_TILE = 256


def _top_rounds(cur, nrounds, on_round, track_rank):
    rank = jnp.full(cur.shape, float(nrounds), F32) if track_rank else None
    for r in range(nrounds):
        m = jnp.max(cur, axis=0, keepdims=True)
        on_round(r, m)
        hit = cur == m
        if track_rank:
            rank = jnp.where(hit, float(r), rank)
        if r + 1 < nrounds:
            cur = jnp.where(hit, NEG_INF, cur)
    return rank


def _peer_route_kernel(h_ref, g_ref, wq_ref, sk_ref, xn_ref, rank1_ref, e1_ref, cnt_ref, e0_ref, qt):
    tp = ROUTE_TILE
    nk = PEER_N_KEYS
    k = PEER_TOPK
    xn = _rms(h_ref[...], g_ref[...]).astype(BF16)
    xn_ref[...] = xn
    qt[...] = _dot_nt(wq_ref[...], xn)
    row16 = lax.broadcasted_iota(jnp.int32, (k, tp), 0)

    def one_head(h):
        tops, scores, ranks = [], [], []
        for p in range(2):
            hp = 2 * h + p
            q_hp = qt[pl.ds(pl.multiple_of(hp * nk, nk), nk), :].astype(BF16)
            s = _dot(sk_ref[hp], q_hp)
            scores.append(s)
            box = [jnp.full((k, tp), NEG_INF, F32)]

            def on_round(r, m, box=box):
                box[0] = jnp.where(row16 == r, m, box[0])

            ranks.append(_top_rounds(s, k, on_round, track_rank=(p == 1)))
            tops.append(box[0])
        t0, t1 = tops
        half = k // 2
        cand = jnp.concatenate([t0[0:1, :] + t1] + [t0[a:a + 1, :] + t1[0:half, :] for a in range(1, half)]
                               + [t0[half:, :] + t1[0:1, :]], axis=0)
        top = t0[0:1, :] + t1[0:1, :]
        acc = [jnp.zeros((1, tp), F32), top]

        def on_cand(r, m, acc=acc):
            acc[0] = acc[0] + jnp.exp(m - top)
            acc[1] = m

        _top_rounds(cand, k, on_cand, track_rank=False)
        z, thr = acc
        n = jnp.zeros((k, tp), F32)
        for b in range(k):
            n = n + jnp.where(t0 + t1[b:b + 1, :] >= thr, 1.0, 0.0)
        cnt = jnp.zeros((nk, tp), F32)
        for a in range(k):
            cnt = jnp.where(scores[0] == t0[a:a + 1, :], n[a:a + 1, :], cnt)
        rank1_ref[h] = ranks[1].astype(BF16)
        e1_ref[h] = jnp.exp(scores[1] - t1[0:1, :]).astype(BF16)
        cnt_ref[h] = cnt
        e0_ref[h] = jnp.exp(scores[0] - t0[0:1, :]) / z

    def head_pair(i, carry):
        one_head(2 * i)
        one_head(2 * i + 1)
        return carry

    lax.fori_loop(0, PEER_HEADS // 2, head_pair, 0)


def _peer_route(h, g, wq_t_bf16, sub_keys_bf16):
    t = h.shape[0]
    tp = ROUTE_TILE
    nk = PEER_N_KEYS
    nh = PEER_HEADS
    row = lambda i: (i, 0)
    lanes = lambda i: (0, 0, i)
    spec = pl.BlockSpec((nh, nk, tp), lanes)
    sd = lambda dt: jax.ShapeDtypeStruct((nh, nk, t), dt)
    return pl.pallas_call(
        _peer_route_kernel,
        grid=(t // tp,),
        in_specs=[pl.BlockSpec((tp, D_MODEL), row), _full((1, D_MODEL)), _full(wq_t_bf16.shape),
                  _full(sub_keys_bf16.shape)],
        out_specs=[pl.BlockSpec((tp, D_MODEL), row), spec, spec, spec, spec],
        out_shape=[jax.ShapeDtypeStruct((t, D_MODEL), BF16), sd(BF16), sd(BF16), sd(F32), sd(F32)],
        scratch_shapes=[pltpu.VMEM((nh * 2 * nk, tp), F32)],
        compiler_params=_cparams("parallel"),
        name="peer_route",
    )(h, g, wq_t_bf16, sub_keys_bf16)


E_TILE = 1024
DENSE_TILE = 512


BF16_ROWS = 16


def _rows_bf16(row, nrows):
    one = jnp.broadcast_to(row, (BF16_ROWS, row.shape[1])).astype(BF16)
    return jnp.concatenate([one] * (nrows // BF16_ROWS), axis=0)


def _peer_dense_kernel(xn_ref, u_ref, vt_ref, rank1_ref, e1_ref, cnt_ref, e0_ref, h_ref, out_ref, acc):
    e = pl.program_id(1)
    nk = PEER_N_KEYS
    tp = xn_ref.shape[0]

    @pl.when(e == 0)
    def _():
        acc[...] = jnp.zeros_like(acc)

    gates = []
    for ii in range(E_TILE // nk):
        i = e * (E_TILE // nk) + ii
        gate = jnp.zeros((nk, tp), BF16)
        for h in range(PEER_HEADS):
            cnt_row = _rows_bf16(cnt_ref[h, pl.ds(i, 1), :], nk)
            e0_row = _rows_bf16(e0_ref[h, pl.ds(i, 1), :], nk)
            gate = gate + jnp.where(rank1_ref[h] < cnt_row, e1_ref[h], jnp.zeros((), BF16)) * e0_row
        gates.append(gate)
    hid = _dot_nt(u_ref[...], xn_ref[...])
    act = (0.5 * hid * (1.0 + lax.erf(hid * (2.0 ** -0.5)))).astype(BF16)
    acc[...] += _dot(vt_ref[...], act * jnp.concatenate(gates, axis=0))

    @pl.when(e == pl.num_programs(1) - 1)
    def _():
        out_ref[...] = h_ref[...] + acc[...].T


def _peer_dense(h, xn, u_bf16, vt_bf16, rank1, e1, cnt, e0):
    t = h.shape[0]
    tp = min(DENSE_TILE, t)
    ne = u_bf16.shape[0]
    nh, nk = PEER_HEADS, PEER_N_KEYS
    tok = lambda i, e: (i, 0)
    lanes = lambda i, e: (0, 0, i)
    spec = pl.BlockSpec((nh, nk, tp), lanes)
    return pl.pallas_call(
        _peer_dense_kernel,
        grid=(t // tp, ne // E_TILE),
        in_specs=[pl.BlockSpec((tp, D_MODEL), tok), pl.BlockSpec((E_TILE, D_MODEL), lambda i, e: (e, 0)),
                  pl.BlockSpec((D_MODEL, E_TILE), lambda i, e: (0, e)), spec, spec, spec, spec,
                  pl.BlockSpec((tp, D_MODEL), tok)],
        out_specs=pl.BlockSpec((tp, D_MODEL), tok),
        out_shape=jax.ShapeDtypeStruct((t, D_MODEL), F32),
        scratch_shapes=[pltpu.VMEM((D_MODEL, tp), F32)],
        compiler_params=_cparams("parallel", "arbitrary"),
        name="peer_dense",
    )(xn, u_bf16, vt_bf16, rank1, e1, cnt, e0, h)


def _ple_kernel(h_ref, p_ref, g_ref, wg_ref, b_ref, wp_ref, fg_ref, out_ref, *, final):
    h = h_ref[...]
    gate = _sigmoid(_dot(_rms(h, g_ref[...]).astype(BF16), wg_ref[...]) + b_ref[...])
    out = h + gate * _dot(p_ref[...].astype(BF16), wp_ref[...])
    out_ref[...] = _rms(out, fg_ref[...]) if final else out


def _ple(h, p, g, wg_bf16, b, wp_bf16, final_g, final, tm=512):
    t = h.shape[0]
    row = lambda i: (i, 0)
    vec = _full((1, D_MODEL))
    return pl.pallas_call(
        functools.partial(_ple_kernel, final=final),
        grid=(t // tm,),
        in_specs=[pl.BlockSpec((tm, D_MODEL), row), pl.BlockSpec((tm, PLE_DIM), row), vec,
                  _full(wg_bf16.shape), vec, _full(wp_bf16.shape), vec],
        out_specs=pl.BlockSpec((tm, D_MODEL), row),
        out_shape=jax.ShapeDtypeStruct((t, D_MODEL), F32),
        compiler_params=_cparams("parallel"),
        name="ple_final" if final else "ple",
    )(h, p, g, wg_bf16, b, wp_bf16, final_g)


def _mixer(h, bsz, seq, lw, consts):
    qs, ks, vs, u = _inproj(h, lw["norm_mix_g"], lw["w_in"], consts["cos"], consts["sin"], seq)
    att_parts = [_attn_pattern(q, k, v, bsz, seq, dil) for q, k, v, dil in zip(qs, ks, vs, DILATIONS)]
    prep = _rwkv_prep(u, bsz, seq, lw["rwkv_shift"], lw["rwkv_w0"], lw["rwkv_w2"], lw["rwkv_a0"],
                      lw["rwkv_a2"], lw["rwkv_g2"], lw["rwkv_k_k"], lw["rwkv_k_a"], lw["rwkv_r_k"], consts)
    dir_ops, (rv, wdec, bonus, g) = prep[:N_DIR_OPERANDS], prep[N_DIR_OPERANDS:]
    yf, yb = _rwkv_scan(dir_ops, rv, wdec, bsz, seq, consts["scan_masks"])
    return _outproj(h, att_parts, lw["att_norm_g"], yf, yb, bonus, g, lw["rwkv_ln_w"], lw["rwkv_ln_b"],
                    consts["bd512"], lw["w_out"])


def _peer(h, lw):
    xn, rank1, e1, cnt, e0 = _peer_route(h, lw["norm_ffn_g"], lw["peer_w_q_t"], lw["peer_sub_keys"])
    return _peer_dense(h, xn, lw["peer_u"], lw["peer_v_t"], rank1, e1, cnt, e0)


def _layer_weights(i, wts):
    row = lambda a: a[i].reshape(1, -1)
    bf = lambda a: a[i].astype(BF16)
    return dict(
        norm_mix_g=row(wts["norm_mix_g"]), w_in=bf(wts["w_in"]), att_norm_g=row(wts["att_norm_g"]),
        rwkv_shift=wts["rwkv_shift"][i], rwkv_w0=wts["rwkv_w0"][i], rwkv_w2=bf(wts["rwkv_w2"]),
        rwkv_a0=wts["rwkv_a0"][i], rwkv_a2=bf(wts["rwkv_a2"]), rwkv_g2=bf(wts["rwkv_g2"]),
        rwkv_k_k=row(wts["rwkv_k_k"]), rwkv_k_a=row(wts["rwkv_k_a"]), rwkv_r_k=row(wts["rwkv_r_k"]),
        rwkv_ln_w=row(wts["rwkv_ln_w"]), rwkv_ln_b=row(wts["rwkv_ln_b"]), w_out=bf(wts["w_out"]),
        norm_ffn_g=row(wts["norm_ffn_g"]), peer_w_q_t=bf(wts["peer_w_q"]).T,
        peer_sub_keys=bf(wts["peer_sub_keys"]).reshape(2 * PEER_HEADS, PEER_N_KEYS, PEER_N_KEYS),
        peer_u=bf(wts["peer_u"]), peer_v_t=bf(wts["peer_v"]).T,
        ple_norm_g=row(wts["ple_norm_g"]), ple_w_gate=bf(wts["ple_w_gate"]), ple_b_gate=row(wts["ple_b_gate"]),
        ple_w_proj=bf(wts["ple_w_proj"]),
    )


def _trunk(x, p, wts, final_g):
    bsz, seq, _ = x.shape
    depth = p.shape[0]
    t = bsz * seq
    consts = _prep_consts()
    consts["cos"], consts["sin"] = _rope_tables(seq)
    consts["scan_masks"] = _scan_masks()
    h = x.reshape(t, D_MODEL)
    fg = final_g.reshape(1, D_MODEL)
    for i in range(depth):
        lw = _layer_weights(i, wts)
        h = _mixer(h, bsz, seq, lw, consts)
        h = _peer(h, lw)
        h = _ple(h, p[i].reshape(t, PLE_DIM), lw["ple_norm_g"], lw["ple_w_gate"], lw["ple_b_gate"],
                 lw["ple_w_proj"], fg, final=(i == depth - 1))
    return h.reshape(bsz, seq, D_MODEL)


def kernel(x_prompt, x_sample, p_prompt, p_sample, norm_mix_g, w_in, att_norm_g, rwkv_shift, rwkv_w0, rwkv_w2, rwkv_a0, rwkv_a2, rwkv_g2, rwkv_k_k, rwkv_k_a, rwkv_r_k, rwkv_ln_w, rwkv_ln_b, w_out, norm_ffn_g, peer_w_q, peer_sub_keys, peer_u, peer_v, ple_norm_g, ple_w_gate, ple_b_gate, ple_w_proj, final_norm_g):
    wts = dict(norm_mix_g=norm_mix_g, w_in=w_in, att_norm_g=att_norm_g, rwkv_shift=rwkv_shift, rwkv_w0=rwkv_w0,
               rwkv_w2=rwkv_w2, rwkv_a0=rwkv_a0, rwkv_a2=rwkv_a2, rwkv_g2=rwkv_g2, rwkv_k_k=rwkv_k_k,
               rwkv_k_a=rwkv_k_a, rwkv_r_k=rwkv_r_k, rwkv_ln_w=rwkv_ln_w, rwkv_ln_b=rwkv_ln_b, w_out=w_out,
               norm_ffn_g=norm_ffn_g, peer_w_q=peer_w_q, peer_sub_keys=peer_sub_keys, peer_u=peer_u,
               peer_v=peer_v, ple_norm_g=ple_norm_g, ple_w_gate=ple_w_gate, ple_b_gate=ple_b_gate,
               ple_w_proj=ple_w_proj)
    nb = x_prompt.shape[0]
    x = jnp.concatenate([x_prompt, x_sample], axis=0)
    p = jnp.concatenate([p_prompt, p_sample], axis=1)
    y = _trunk(x, p, wts, final_norm_g)
    return (y[:nb], y[nb:])
```

```python
import functools

import jax
import jax.numpy as jnp
import numpy as np
from jax import lax
from jax.experimental import pallas as pl
from jax.experimental.pallas import tpu as pltpu

F32 = jnp.float32
BF16 = jnp.bfloat16

D_MODEL = 1024
HEAD_DIM = 64
ATT_WIDTH = 512
ATT_HEADS = 8
RWKV_WIDTH = 512
RWKV_COLS = 1920
QKV_COLS = 3 * ATT_WIDTH
DILATIONS = (1, 4, 16)
HALF_WINDOW = 64
ROPE_THETA = 500000.0
ROPE_DIM = 16
NORM_EPS = 1e-6
LN_X_EPS = 64e-5
NEG_INF = -1e30
PEER_HEADS = 8
PEER_N_KEYS = 128
PEER_TOPK = 16
PLE_DIM = 256
CHUNK = 64
GROUP = 256

VMEM_LIMIT = 56 * 1024 * 1024


def _cparams(*sem, flags=None):
    return pltpu.CompilerParams(dimension_semantics=sem, vmem_limit_bytes=VMEM_LIMIT, flags=flags)


def _full(shape):
    nd = len(shape)
    return pl.BlockSpec(shape, lambda *_: (0,) * nd)


def _dot(a, b):
    return jnp.dot(a, b, preferred_element_type=F32)


def _dot_nt(a, b):
    return lax.dot_general(a, b, (((1,), (1,)), ((), ())), preferred_element_type=F32)


def _dot_tn(a, b):
    return lax.dot_general(a, b, (((0,), (0,)), ((), ())), preferred_element_type=F32)


def _split3(x):
    hi = x.astype(BF16)
    r1 = x - hi.astype(F32)
    mid = r1.astype(BF16)
    lo = (r1 - mid.astype(F32)).astype(BF16)
    return hi, mid, lo


def _dot_lexact(l_bf16, x):
    hi, mid, lo = _split3(x)
    return _dot(l_bf16, hi) + _dot(l_bf16, mid) + _dot(l_bf16, lo)


def _dot_rexact(x, r_bf16):
    hi, mid, lo = _split3(x)
    return _dot(hi, r_bf16) + _dot(mid, r_bf16) + _dot(lo, r_bf16)


def _rms(x, g):
    return x * lax.rsqrt(jnp.mean(x * x, axis=-1, keepdims=True) + NORM_EPS) * g


def _sigmoid(x):
    return 1.0 / (1.0 + jnp.exp(-x))


LANES = 128
ATT_LANE_TILES = ATT_WIDTH // LANES


def _to_classes(t, stage, out_refs):
    rows = t.shape[0]
    for c in range(ATT_LANE_TILES):
        stage[c] = t[:, c * LANES:(c + 1) * LANES]
    for dil, ref in zip(DILATIONS, out_refs):
        if dil == 1:
            ref[...] = t.astype(ref.dtype)
            continue
        for r in range(dil):
            for c in range(ATT_LANE_TILES):
                lo = r * ATT_WIDTH + c * LANES
                ref[:, lo:lo + LANES] = stage[c, pl.ds(r, rows // dil, stride=dil), :].astype(ref.dtype)


def _from_classes(ref, dil, stage):
    if dil == 1:
        return ref[...]
    rows = ref.shape[0] * dil
    for r in range(dil):
        for c in range(ATT_LANE_TILES):
            lo = r * ATT_WIDTH + c * LANES
            stage[c, pl.ds(r, rows // dil, stride=dil), :] = ref[:, lo:lo + LANES]
    return jnp.concatenate([stage[c] for c in range(ATT_LANE_TILES)], axis=1)


def _inproj_kernel(h_ref, g_ref, w_ref, cos_ref, sin_ref, *refs):
    nd = len(DILATIONS)
    q_refs, k_refs, v_refs = refs[:nd], refs[nd:2 * nd], refs[2 * nd:3 * nd]
    u_ref, stage = refs[3 * nd:]
    xn = _rms(h_ref[...], g_ref[...]).astype(BF16)
    qkv = _dot(xn, w_ref[:, :QKV_COLS])
    u_ref[...] = _dot(xn, w_ref[:, QKV_COLS:])
    cos = cos_ref[...]
    sin = sin_ref[...]
    lane = lax.broadcasted_iota(jnp.int32, cos.shape, 1) % HEAD_DIM
    half = ROPE_DIM // 2

    def rot(t):
        partner = jnp.where(lane < half, pltpu.roll(t, ATT_WIDTH - half, 1), pltpu.roll(t, half, 1))
        return t * cos + partner * sin

    _to_classes(rot(qkv[:, :ATT_WIDTH]) * (HEAD_DIM ** -0.5), stage, q_refs)
    _to_classes(rot(qkv[:, ATT_WIDTH:2 * ATT_WIDTH]), stage, k_refs)
    _to_classes(qkv[:, 2 * ATT_WIDTH:], stage, v_refs)


def _inproj(h, g, w_bf16, cos_t, sin_t, seq, tm=512):
    t = h.shape[0]
    tm = min(tm, seq)
    nseq = seq // tm
    row = lambda i: (i, 0)
    pos = lambda i: (i % nseq, 0)
    cls_specs = [pl.BlockSpec((tm // d, d * ATT_WIDTH), row) for d in DILATIONS]
    cls_shapes = [jax.ShapeDtypeStruct((t // d, d * ATT_WIDTH), BF16) for d in DILATIONS]
    outs = pl.pallas_call(
        _inproj_kernel,
        grid=(t // tm,),
        in_specs=[pl.BlockSpec((tm, D_MODEL), row), _full((1, D_MODEL)), _full(w_bf16.shape),
                  pl.BlockSpec((tm, ATT_WIDTH), pos), pl.BlockSpec((tm, ATT_WIDTH), pos)],
        out_specs=cls_specs * 3 + [pl.BlockSpec((tm, RWKV_COLS), row)],
        out_shape=cls_shapes * 3 + [jax.ShapeDtypeStruct((t, RWKV_COLS), F32)],
        scratch_shapes=[pltpu.VMEM((ATT_LANE_TILES, tm, LANES), F32)],
        compiler_params=_cparams("parallel"),
        name="inproj",
    )(h, g, w_bf16, cos_t, sin_t)
    nd = len(DILATIONS)
    return outs[:nd], outs[nd:2 * nd], outs[2 * nd:3 * nd], outs[3 * nd]


def _rope_tables(seq):
    half = ROPE_DIM // 2
    inv_freq = ROPE_THETA ** (-jnp.arange(half, dtype=F32) / half)
    ang = jnp.arange(seq, dtype=F32)[:, None] * inv_freq[None, :]
    ones = jnp.ones((seq, HEAD_DIM - ROPE_DIM), F32)
    cos_h = jnp.concatenate([jnp.cos(ang), jnp.cos(ang), ones], axis=1)
    sin_h = jnp.concatenate([-jnp.sin(ang), jnp.sin(ang), 0.0 * ones], axis=1)
    return jnp.tile(cos_h, (1, ATT_HEADS)), jnp.tile(sin_h, (1, ATT_HEADS))


ATT_SUB = 128


def _attn_kernel(q_ref, kp_ref, kc_ref, kn_ref, vp_ref, vc_ref, vn_ref, o_ref, lse_ref,
                 kbuf, vbuf, *, length, tq):
    j = pl.program_id(2)
    hw = HALF_WINDOW
    kbuf[0:hw] = kp_ref[0]
    kbuf[hw:hw + tq] = kc_ref[0]
    kbuf[hw + tq:] = kn_ref[0]
    vbuf[0:hw] = vp_ref[0]
    vbuf[hw:hw + tq] = vc_ref[0]
    vbuf[hw + tq:] = vn_ref[0]
    nk = ATT_SUB + 2 * hw
    qi = lax.broadcasted_iota(jnp.int32, (ATT_SUB, nk), 0)
    ki = lax.broadcasted_iota(jnp.int32, (ATT_SUB, nk), 1)
    band = jnp.abs(ki - hw - qi) <= hw
    heads = [slice(h * HEAD_DIM, (h + 1) * HEAD_DIM) for h in range(ATT_HEADS)]
    blocks = [(s * ATT_SUB, c) for s in range(tq // ATT_SUB) for c in heads]
    sc = jnp.concatenate([_dot_nt(q_ref[0, q0:q0 + ATT_SUB, c], kbuf[q0:q0 + nk, c]) for q0, c in blocks], axis=0)
    valid = []
    for s in range(tq // ATT_SUB):
        kpos = j * tq + (s * ATT_SUB - hw) + ki
        valid += [band & (kpos >= 0) & (kpos < length)] * ATT_HEADS
    sc = jnp.where(jnp.concatenate(valid, axis=0), sc, NEG_INF)
    m = jnp.max(sc, axis=-1, keepdims=True)
    p = jnp.exp(sc - m)
    l = jnp.sum(p, axis=-1, keepdims=True)
    pb = p.astype(BF16)
    inv_l = 1.0 / l
    lse = m + jnp.log(l)
    for n, (q0, c) in enumerate(blocks):
        rows = slice(n * ATT_SUB, (n + 1) * ATT_SUB)
        o_ref[0, q0:q0 + ATT_SUB, c] = _dot(pb[rows], vbuf[q0:q0 + nk, c]) * inv_l[rows]
        lse_ref[0, q0:q0 + ATT_SUB, c] = jnp.broadcast_to(lse[rows], (ATT_SUB, HEAD_DIM))


def _attn_pattern(q, k, v, bsz, seq, dil):
    length = seq // dil
    tq = min(length, 512)
    nq = length // tq
    nh = length // HALF_WINDOW
    rq = tq // HALF_WINDOW
    view = lambda a: a.reshape(bsz, length, dil * ATT_WIDTH)
    cur = pl.BlockSpec((1, tq, ATT_WIDTH), lambda b, r, j: (b, j, r))
    prev = pl.BlockSpec((1, HALF_WINDOW, ATT_WIDTH), lambda b, r, j: (b, jnp.maximum(j * rq - 1, 0), r))
    nxt = pl.BlockSpec((1, HALF_WINDOW, ATT_WIDTH), lambda b, r, j: (b, jnp.minimum((j + 1) * rq, nh - 1), r))
    out_sd = jax.ShapeDtypeStruct((bsz, length, dil * ATT_WIDTH), F32)
    o, lse = pl.pallas_call(
        functools.partial(_attn_kernel, length=length, tq=tq),
        grid=(bsz, dil, nq),
        in_specs=[cur, prev, cur, nxt, prev, cur, nxt],
        out_specs=[cur, cur],
        out_shape=[out_sd, out_sd],
        scratch_shapes=[pltpu.VMEM((tq + 2 * HALF_WINDOW, ATT_WIDTH), BF16)] * 2,
        compiler_params=_cparams("parallel", "parallel", "parallel"),
        name=f"attn_d{dil}",
    )(view(q), view(k), view(k), view(k), view(v), view(v), view(v))
    flat = (bsz * length, dil * ATT_WIDTH)
    return o.reshape(flat), lse.reshape(flat)


R_TILE = 256
R_COLS = dict(r=0, k=512, v=1024, wd=1536, ad=1664, gd=1792)


def _rwkv_prep_kernel(uc_ref, up_ref, un_ref, shift_ref, w0_ref, w2_ref, a0_ref, a2_ref, g2_ref,
                      kk_ref, ka_ref, rk_ref, bd_ref, tri_ref, sel_ref,
                      rt_ref, at_ref, bt_ref, kt_ref, bh_ref, kh_ref, v_ref, wdec_ref, bonus_ref, g_ref):
    j = pl.program_id(1)
    nj = pl.num_programs(1)
    tr = R_TILE
    cur = uc_ref[0]
    row = lax.broadcasted_iota(jnp.int32, cur.shape, 0)
    prev_edge = up_ref[0, 7:8, :] * (j > 0).astype(F32)
    next_edge = un_ref[0, 0:1, :] * (j < nj - 1).astype(F32)
    prev = jnp.where(row == 0, prev_edge, pltpu.roll(cur, 1, 0))
    nxt = jnp.where(row == tr - 1, next_edge, pltpu.roll(cur, tr - 1, 0))
    us = shift_ref[0:1, :] * prev + shift_ref[1:2, :] * cur + shift_ref[2:3, :] * nxt

    w = RWKV_WIDTH
    r = us[:, R_COLS["r"]:R_COLS["r"] + w]
    k = us[:, R_COLS["k"]:R_COLS["k"] + w]
    v = us[:, R_COLS["v"]:R_COLS["v"] + w]
    gd = us[:, R_COLS["gd"]:]
    bd = bd_ref[...]

    g_ref[...] = _dot(_sigmoid(gd).astype(BF16), g2_ref[...])
    v_ref[...] = v.astype(BF16)
    kk = k * kk_ref[...]
    kk = kk / jnp.maximum(jnp.sqrt(_dot_rexact(kk * kk, bd)), 1e-12)

    nchunk = tr // CHUNK
    ksum = jnp.zeros_like(k)
    for d in range(2):
        wd = us[:, R_COLS["wd"] + 64 * d:R_COLS["wd"] + 64 * (d + 1)]
        ad = us[:, R_COLS["ad"] + 64 * d:R_COLS["ad"] + 64 * (d + 1)]
        z = w0_ref[d:d + 1, :] + _dot(jnp.tanh(wd).astype(BF16), w2_ref[d])
        w_log = -(jnp.maximum(-z, 0.0) + jnp.log(1.0 + jnp.exp(-jnp.abs(z)))) - 0.5
        lw = -jnp.exp(w_log)
        a = _sigmoid(a0_ref[d:d + 1, :] + _dot(ad.astype(BF16), a2_ref[d]))
        kd = k * (1.0 + (a - 1.0) * ka_ref[...])
        ksum = ksum + kd
        b = kk * a
        c = _dot_lexact(tri_ref[d], lw)
        tot = _dot_lexact(sel_ref[...], lw)
        cend = jnp.concatenate(
            [jnp.broadcast_to(tot[i:i + 1, :], (CHUNK, w)) for i in range(nchunk)], axis=0)
        e_c = jnp.exp(c)
        e_nc = jnp.exp(-c)
        e_rem = jnp.exp(cend - c)
        rt_ref[d] = (r * e_c).astype(BF16)
        at_ref[d] = (-kk * jnp.exp(c - lw)).astype(BF16)
        bt_ref[d] = (b * e_nc).astype(BF16)
        kt_ref[d] = (kd * e_nc).astype(BF16)
        bh_ref[d] = (b * e_rem).astype(BF16)
        kh_ref[d] = (kd * e_rem).astype(BF16)
        wdec = jnp.exp(tot)
        for i in range(nchunk):
            wdec_ref[d, 0, i] = wdec[i:i + 1, :]
    bonus_ref[...] = _dot_rexact(r * ksum * rk_ref[...], bd) * v


def _prep_consts():
    tr = R_TILE
    idx = np.arange(tr)
    same = (idx[:, None] // CHUNK) == (idx[None, :] // CHUNK)
    tri = np.stack([same & (idx[:, None] >= idx[None, :]), same & (idx[:, None] <= idx[None, :])])
    sel = (np.arange(8)[:, None] == (idx[None, :] // CHUNK))
    hd = np.arange(RWKV_WIDTH) // HEAD_DIM
    bd = hd[:, None] == hd[None, :]
    as_bf16 = lambda m: jnp.asarray(m.astype(np.float32), dtype=BF16)
    return dict(tri=as_bf16(tri), sel=as_bf16(sel), bd512=as_bf16(bd))


def _rwkv_prep(u, bsz, seq, shift_w, w0, w2, a0, a2, g2, k_k, k_a, r_k, consts):
    tr = R_TILE
    nj = seq // tr
    t = bsz * seq
    w = RWKV_WIDTH
    u3 = u.reshape(bsz, seq, RWKV_COLS)
    rows8 = tr // 8
    cur = pl.BlockSpec((1, tr, RWKV_COLS), lambda b, j: (b, j, 0))
    prev = pl.BlockSpec((1, 8, RWKV_COLS), lambda b, j: (b, jnp.maximum(j * rows8 - 1, 0), 0))
    nxt = pl.BlockSpec((1, 8, RWKV_COLS), lambda b, j: (b, jnp.minimum((j + 1) * rows8, seq // 8 - 1), 0))
    tok2 = pl.BlockSpec((2, tr, w), lambda b, j: (0, b * nj + j, 0))
    tok = pl.BlockSpec((tr, w), lambda b, j: (b * nj + j, 0))
    nchunk = tr // CHUNK
    wdec_spec = pl.BlockSpec((2, 1, nchunk, 1, w), lambda b, j: (0, b, j, 0, 0))
    small = [shift_w, w0, w2, a0, a2, g2, k_k, k_a, r_k, consts["bd512"], consts["tri"], consts["sel"]]
    sd2 = jax.ShapeDtypeStruct((2, t, w), BF16)
    return pl.pallas_call(
        _rwkv_prep_kernel,
        grid=(bsz, nj),
        in_specs=[cur, prev, nxt] + [_full(a.shape) for a in small],
        out_specs=[tok2] * 6 + [tok, wdec_spec, tok, tok],
        out_shape=[sd2] * 6 + [jax.ShapeDtypeStruct((t, w), BF16),
                               jax.ShapeDtypeStruct((2, bsz, seq // CHUNK, 1, w), F32),
                               jax.ShapeDtypeStruct((t, w), F32), jax.ShapeDtypeStruct((t, w), F32)],
        compiler_params=_cparams("parallel", "parallel"),
        name="rwkv_prep",
    )(u3, u3, u3, *small)


S_TILE = 256
LEVELS = (2, 4, 8, 16, 32)
N_DIR_OPERANDS = 6


def _rwkv_scan_kernel(*refs):
    nop = N_DIR_OPERANDS
    dir_refs = refs[:2 * nop]
    vf_ref, vb_ref, wf_ref, wb_ref, mask_ref, yf_ref, yb_ref, state = refs[2 * nop:]
    j = pl.program_id(2)

    @pl.when(j == 0)
    def _():
        state[...] = jnp.zeros_like(state)

    nchunk = S_TILE // CHUNK
    nh = GROUP // HEAD_DIM
    bdm = mask_ref[0]
    bdm_f32 = bdm.astype(F32)
    eye = mask_ref[1].astype(F32)
    tile4 = lambda x: jnp.concatenate([x] * nh, axis=0)
    io = ((vf_ref, wf_ref, yf_ref), (vb_ref, wb_ref, yb_ref))
    order = [(d, ci if d == 0 else nchunk - 1 - ci) for ci in range(nchunk) for d in range(2)]

    pre = {}
    for d, c in order:
        rt_ref, at_ref, bt_ref, kt_ref, bh_ref, kh_ref = [dir_refs[2 * i + d] for i in range(nop)]
        rows = slice(c * CHUNK, (c + 1) * CHUNK)
        a_bd = tile4(at_ref[0, rows, :]) * bdm
        r_bd = tile4(rt_ref[0, rows, :]) * bdm
        v_bd = tile4(io[d][0][0, rows, :]) * bdm
        ar = jnp.concatenate([a_bd, r_bd], axis=0)
        ar_b = _dot_nt(ar, tile4(bt_ref[0, rows, :]))
        ar_k = _dot_nt(ar, tile4(kt_ref[0, rows, :]))
        strict = mask_ref[2 + d].astype(F32)
        incl = mask_ref[4 + d].astype(F32)
        a_ab = ar_b[:GROUP] * strict
        a_ak = (ar_k[:GROUP] * strict).astype(BF16)
        a_r = jnp.concatenate([(ar_b[GROUP:] * incl).astype(BF16), (ar_k[GROUP:] * incl).astype(BF16)], axis=1)
        hat = jnp.concatenate([tile4(bh_ref[0, rows, :]), tile4(kh_ref[0, rows, :])], axis=0)
        pre[d, c] = dict(a_bd=a_bd, r_bd=r_bd, v_bd=v_bd, a_ab=a_ab.astype(BF16), a_ak=a_ak, a_r=a_r, hat=hat,
                         tinv=eye + a_ab * mask_ref[6 + d].astype(F32))
    for li in range(len(LEVELS)):
        for d, c in order:
            p = pre[d, c]
            lm = p["a_ab"] * mask_ref[8 + 2 * li + d]
            tb = p["tinv"].astype(BF16)
            p["tinv"] = p["tinv"] + _dot(tb, _dot(lm, tb).astype(BF16))
    s = [state[0], state[1]]
    for d, c in order:
        p = pre[d, c]
        w_ref, y_ref = io[d][1], io[d][2]
        rows = slice(c * CHUNK, (c + 1) * CHUNK)
        s_bf = s[d].astype(BF16)
        rhs = _dot_nt(p["a_bd"], s_bf) + _dot(p["a_ak"], p["v_bd"])
        u_bf = _dot(p["tinv"].astype(BF16), rhs.astype(BF16)).astype(BF16)
        uv = jnp.concatenate([u_bf, p["v_bd"]], axis=0)
        y = _dot_nt(p["r_bd"], s_bf) + _dot(p["a_r"], uv)
        y_nat = y[0:CHUNK]
        for hh in range(1, nh):
            y_nat = y_nat + y[hh * CHUNK:(hh + 1) * CHUNK]
        y_ref[0, rows, :] = y_nat
        s[d] = s[d] * w_ref[0, 0, c] + _dot_tn(uv, p["hat"]) * bdm_f32
    state[0] = s[0]
    state[1] = s[1]


def _scan_masks():
    n = GROUP
    idx = np.arange(n)
    head = idx // CHUNK
    t = idx % CHUNK
    same = head[:, None] == head[None, :]
    tt, ss = t[:, None], t[None, :]
    masks = [same, np.eye(n, dtype=bool)]
    masks += [same & (tt > ss), same & (tt < ss)]
    masks += [same & (tt >= ss), same & (tt <= ss)]
    for m in (1,) + LEVELS:
        blk = same & (tt // (2 * m) == ss // (2 * m))
        lower = blk & ((tt // m) % 2 == 1) & ((ss // m) % 2 == 0)
        upper = blk & ((tt // m) % 2 == 0) & ((ss // m) % 2 == 1)
        masks += [lower, upper]
    return jnp.asarray(np.stack(masks).astype(np.float32), dtype=BF16)


def _rwkv_scan(dir_ops, v, wdec, bsz, seq, masks):
    st = S_TILE
    nj = seq // st
    ng = RWKV_WIDTH // GROUP
    w = RWKV_WIDTH
    t = bsz * seq
    nchunk = st // CHUNK
    blk = (1, st, GROUP)
    fwd = lambda d: pl.BlockSpec(blk, lambda b, g, j: (d, b * nj + j, g))
    bwd = lambda d: pl.BlockSpec(blk, lambda b, g, j: (d, b * nj + (nj - 1 - j), g))
    wf = pl.BlockSpec((1, 1, nchunk, 1, GROUP), lambda b, g, j: (0, b, j, 0, g))
    wb = pl.BlockSpec((1, 1, nchunk, 1, GROUP), lambda b, g, j: (1, b, nj - 1 - j, 0, g))
    v3 = v.reshape(1, t, w)
    in_specs, args = [], []
    for arr in dir_ops:
        in_specs += [fwd(0), bwd(1)]
        args += [arr, arr]
    in_specs += [fwd(0), bwd(0), wf, wb, _full(masks.shape)]
    args += [v3, v3, wdec, wdec, masks]
    sd = jax.ShapeDtypeStruct((1, t, w), F32)
    yf, yb = pl.pallas_call(
        _rwkv_scan_kernel,
        grid=(bsz, ng, nj),
        in_specs=in_specs,
        out_specs=[fwd(0), bwd(0)],
        out_shape=[sd, sd],
        scratch_shapes=[pltpu.VMEM((2, GROUP, GROUP), F32)],
        compiler_params=_cparams("parallel", "parallel", "arbitrary"),
        name="rwkv_scan",
    )(*args)
    return yf.reshape(t, w), yb.reshape(t, w)


def _outproj_kernel(h_ref, o1_ref, o2_ref, o3_ref, l1_ref, l2_ref, l3_ref, attg_ref,
                    yf_ref, yb_ref, bonus_ref, g_ref, lnw_ref, lnb_ref, bd_ref, w_ref, out_ref, stage):
    bd = bd_ref[...]
    inv_n = 1.0 / HEAD_DIM
    nd = len(DILATIONS)
    o1, o2, o3 = [_from_classes(r, d, stage.at[i])
                  for i, (r, d) in enumerate(zip((o1_ref, o2_ref, o3_ref), DILATIONS))]
    l1, l2, l3 = [_from_classes(r, d, stage.at[nd + i])
                  for i, (r, d) in enumerate(zip((l1_ref, l2_ref, l3_ref), DILATIONS))]
    m = jnp.maximum(jnp.maximum(l1, l2), l3)
    w1, w2, w3 = jnp.exp(l1 - m), jnp.exp(l2 - m), jnp.exp(l3 - m)
    o = (w1 * o1 + w2 * o2 + w3 * o3) / (w1 + w2 + w3)
    att = o * lax.rsqrt(_dot_rexact(o * o, bd) * inv_n + NORM_EPS) * attg_ref[...]
    y = yf_ref[...] + yb_ref[...]
    yc = y - _dot_rexact(y, bd) * inv_n
    var = _dot_rexact(yc * yc, bd) * inv_n
    yn = yc * lax.rsqrt(var + LN_X_EPS) * lnw_ref[...] + lnb_ref[...]
    rw = (yn + bonus_ref[...]) * g_ref[...]
    out_ref[...] = (h_ref[...] + _dot(att.astype(BF16), w_ref[:ATT_WIDTH, :])
                    + _dot(rw.astype(BF16), w_ref[ATT_WIDTH:, :]))


def _outproj(h, att_parts, att_g, yf, yb, bonus, g, ln_w, ln_b, bd512, w_out_bf16, tm=256):
    t = h.shape[0]
    row = lambda i: (i, 0)
    half = pl.BlockSpec((tm, ATT_WIDTH), row)
    cls = [pl.BlockSpec((tm // d, d * ATT_WIDTH), row) for d in DILATIONS]
    (o1, l1), (o2, l2), (o3, l3) = att_parts
    vec = _full((1, ATT_WIDTH))
    return pl.pallas_call(
        _outproj_kernel,
        grid=(t // tm,),
        in_specs=[pl.BlockSpec((tm, D_MODEL), row)] + cls * 2 + [vec] + [half] * 4 + [vec, vec,
                  _full(bd512.shape), _full(w_out_bf16.shape)],
        out_specs=pl.BlockSpec((tm, D_MODEL), row),
        out_shape=jax.ShapeDtypeStruct((t, D_MODEL), F32),
        scratch_shapes=[pltpu.VMEM((2 * len(DILATIONS), ATT_LANE_TILES, tm, LANES), F32)],
        compiler_params=_cparams("parallel"),
        name="outproj",
    )(h, o1, o2, o3, l1, l2, l3, att_g, yf, yb, bonus, g, ln_w, ln_b, bd512, w_out_bf16)


ROUTE_TILE = 256


def _top_rounds(cur, nrounds, on_round, track_rank):
    rank = jnp.full(cur.shape, float(nrounds), F32) if track_rank else None
    for r in range(nrounds):
        m = jnp.max(cur, axis=0, keepdims=True)
        on_round(r, m)
        hit = cur == m
        if track_rank:
            rank = jnp.where(hit, float(r), rank)
        if r + 1 < nrounds:
            cur = jnp.where(hit, NEG_INF, cur)
    return rank


def _peer_route_kernel(h_ref, g_ref, wq_ref, sk_ref, xn_ref, rank1_ref, e1_ref, cnt_ref, e0_ref, qt):
    tp = ROUTE_TILE
    nk = PEER_N_KEYS
    k = PEER_TOPK
    xn = _rms(h_ref[...], g_ref[...]).astype(BF16)
    xn_ref[...] = xn
    qt[...] = _dot_nt(wq_ref[...], xn)
    row16 = lax.broadcasted_iota(jnp.int32, (k, tp), 0)

    def one_head(h):
        tops, scores, ranks = [], [], []
        for p in range(2):
            hp = 2 * h + p
            q_hp = qt[pl.ds(pl.multiple_of(hp * nk, nk), nk), :].astype(BF16)
            s = _dot(sk_ref[hp], q_hp)
            scores.append(s)
            box = [jnp.full((k, tp), NEG_INF, F32)]

            def on_round(r, m, box=box):
                box[0] = jnp.where(row16 == r, m, box[0])

            ranks.append(_top_rounds(s, k, on_round, track_rank=(p == 1)))
            tops.append(box[0])
        t0, t1 = tops
        half = k // 2
        cand = jnp.concatenate([t0[0:1, :] + t1] + [t0[a:a + 1, :] + t1[0:half, :] for a in range(1, half)]
                               + [t0[half:, :] + t1[0:1, :]], axis=0)
        top = t0[0:1, :] + t1[0:1, :]
        acc = [jnp.zeros((1, tp), F32), top]

        def on_cand(r, m, acc=acc):
            acc[0] = acc[0] + jnp.exp(m - top)
            acc[1] = m

        _top_rounds(cand, k, on_cand, track_rank=False)
        z, thr = acc
        n = jnp.zeros((k, tp), F32)
        for b in range(k):
            n = n + jnp.where(t0 + t1[b:b + 1, :] >= thr, 1.0, 0.0)
        cnt = jnp.zeros((nk, tp), F32)
        for a in range(k):
            cnt = jnp.where(scores[0] == t0[a:a + 1, :], n[a:a + 1, :], cnt)
        rank1_ref[h] = ranks[1].astype(BF16)
        e1_ref[h] = jnp.exp(scores[1] - t1[0:1, :]).astype(BF16)
        cnt_ref[h] = cnt
        e0_ref[h] = jnp.exp(scores[0] - t0[0:1, :]) / z

    def head_pair(i, carry):
        one_head(2 * i)
        one_head(2 * i + 1)
        return carry

    lax.fori_loop(0, PEER_HEADS // 2, head_pair, 0)


def _peer_route(h, g, wq_t_bf16, sub_keys_bf16):
    t = h.shape[0]
    tp = ROUTE_TILE
    nk = PEER_N_KEYS
    nh = PEER_HEADS
    row = lambda i: (i, 0)
    lanes = lambda i: (0, 0, i)
    spec = pl.BlockSpec((nh, nk, tp), lanes)
    sd = lambda dt: jax.ShapeDtypeStruct((nh, nk, t), dt)
    return pl.pallas_call(
        _peer_route_kernel,
        grid=(t // tp,),
        in_specs=[pl.BlockSpec((tp, D_MODEL), row), _full((1, D_MODEL)), _full(wq_t_bf16.shape),
                  _full(sub_keys_bf16.shape)],
        out_specs=[pl.BlockSpec((tp, D_MODEL), row), spec, spec, spec, spec],
        out_shape=[jax.ShapeDtypeStruct((t, D_MODEL), BF16), sd(BF16), sd(BF16), sd(F32), sd(F32)],
        scratch_shapes=[pltpu.VMEM((nh * 2 * nk, tp), F32)],
        compiler_params=_cparams("parallel"),
        name="peer_route",
    )(h, g, wq_t_bf16, sub_keys_bf16)


E_TILE = 2048
DENSE_TILE = 512


BF16_ROWS = 16


def _rows_bf16(row, nrows):
    one = jnp.broadcast_to(row, (BF16_ROWS, row.shape[1])).astype(BF16)
    return jnp.concatenate([one] * (nrows // BF16_ROWS), axis=0)


def _peer_dense_kernel(xn_ref, u_ref, vt_ref, rank1_ref, e1_ref, cnt_ref, e0_ref, h_ref, out_ref, acc):
    e = pl.program_id(1)
    nk = PEER_N_KEYS
    tp = xn_ref.shape[0]

    @pl.when(e == 0)
    def _():
        acc[...] = jnp.zeros_like(acc)

    gates = []
    for ii in range(E_TILE // nk):
        i = e * (E_TILE // nk) + ii
        gate = jnp.zeros((nk, tp), BF16)
        for h in range(PEER_HEADS):
            cnt_row = _rows_bf16(cnt_ref[h, pl.ds(i, 1), :], nk)
            e0_row = _rows_bf16(e0_ref[h, pl.ds(i, 1), :], nk)
            gate = gate + jnp.where(rank1_ref[h] < cnt_row, e1_ref[h], jnp.zeros((), BF16)) * e0_row
        gates.append(gate)
    hid = _dot_nt(u_ref[...], xn_ref[...])
    act = (0.5 * hid * (1.0 + lax.erf(hid * (2.0 ** -0.5)))).astype(BF16)
    acc[...] += _dot(vt_ref[...], act * jnp.concatenate(gates, axis=0))

    @pl.when(e == pl.num_programs(1) - 1)
    def _():
        out_ref[...] = h_ref[...] + acc[...].T


def _peer_dense(h, xn, u_bf16, vt_bf16, rank1, e1, cnt, e0):
    t = h.shape[0]
    tp = min(DENSE_TILE, t)
    ne = u_bf16.shape[0]
    nh, nk = PEER_HEADS, PEER_N_KEYS
    tok = lambda i, e: (i, 0)
    lanes = lambda i, e: (0, 0, i)
    spec = pl.BlockSpec((nh, nk, tp), lanes)
    return pl.pallas_call(
        _peer_dense_kernel,
        grid=(t // tp, ne // E_TILE),
        in_specs=[pl.BlockSpec((tp, D_MODEL), tok), pl.BlockSpec((E_TILE, D_MODEL), lambda i, e: (e, 0)),
                  pl.BlockSpec((D_MODEL, E_TILE), lambda i, e: (0, e)), spec, spec, spec, spec,
                  pl.BlockSpec((tp, D_MODEL), tok)],
        out_specs=pl.BlockSpec((tp, D_MODEL), tok),
        out_shape=jax.ShapeDtypeStruct((t, D_MODEL), F32),
        scratch_shapes=[pltpu.VMEM((D_MODEL, tp), F32)],
        compiler_params=_cparams("parallel", "arbitrary"),
        name="peer_dense",
    )(xn, u_bf16, vt_bf16, rank1, e1, cnt, e0, h)


def _ple_kernel(h_ref, p_ref, g_ref, wg_ref, b_ref, wp_ref, fg_ref, out_ref, *, final):
    h = h_ref[...]
    gate = _sigmoid(_dot(_rms(h, g_ref[...]).astype(BF16), wg_ref[...]) + b_ref[...])
    out = h + gate * _dot(p_ref[...].astype(BF16), wp_ref[...])
    out_ref[...] = _rms(out, fg_ref[...]) if final else out


def _ple(h, p, g, wg_bf16, b, wp_bf16, final_g, final, tm=512):
    t = h.shape[0]
    row = lambda i: (i, 0)
    vec = _full((1, D_MODEL))
    return pl.pallas_call(
        functools.partial(_ple_kernel, final=final),
        grid=(t // tm,),
        in_specs=[pl.BlockSpec((tm, D_MODEL), row), pl.BlockSpec((tm, PLE_DIM), row), vec,
                  _full(wg_bf16.shape), vec, _full(wp_bf16.shape), vec],
        out_specs=pl.BlockSpec((tm, D_MODEL), row),
        out_shape=jax.ShapeDtypeStruct((t, D_MODEL), F32),
        compiler_params=_cparams("parallel"),
        name="ple_final" if final else "ple",
    )(h, p, g, wg_bf16, b, wp_bf16, final_g)


def _mixer(h, bsz, seq, lw, consts):
    qs, ks, vs, u = _inproj(h, lw["norm_mix_g"], lw["w_in"], consts["cos"], consts["sin"], seq)
    att_parts = [_attn_pattern(q, k, v, bsz, seq, dil) for q, k, v, dil in zip(qs, ks, vs, DILATIONS)]
    prep = _rwkv_prep(u, bsz, seq, lw["rwkv_shift"], lw["rwkv_w0"], lw["rwkv_w2"], lw["rwkv_a0"],
                      lw["rwkv_a2"], lw["rwkv_g2"], lw["rwkv_k_k"], lw["rwkv_k_a"], lw["rwkv_r_k"], consts)
    dir_ops, (rv, wdec, bonus, g) = prep[:N_DIR_OPERANDS], prep[N_DIR_OPERANDS:]
    yf, yb = _rwkv_scan(dir_ops, rv, wdec, bsz, seq, consts["scan_masks"])
    return _outproj(h, att_parts, lw["att_norm_g"], yf, yb, bonus, g, lw["rwkv_ln_w"], lw["rwkv_ln_b"],
                    consts["bd512"], lw["w_out"])


def _peer(h, lw):
    xn, rank1, e1, cnt, e0 = _peer_route(h, lw["norm_ffn_g"], lw["peer_w_q_t"], lw["peer_sub_keys"])
    return _peer_dense(h, xn, lw["peer_u"], lw["peer_v_t"], rank1, e1, cnt, e0)


def _layer_weights(i, wts):
    row = lambda a: a[i].reshape(1, -1)
    bf = lambda a: a[i].astype(BF16)
    return dict(
        norm_mix_g=row(wts["norm_mix_g"]), w_in=bf(wts["w_in"]), att_norm_g=row(wts["att_norm_g"]),
        rwkv_shift=wts["rwkv_shift"][i], rwkv_w0=wts["rwkv_w0"][i], rwkv_w2=bf(wts["rwkv_w2"]),
        rwkv_a0=wts["rwkv_a0"][i], rwkv_a2=bf(wts["rwkv_a2"]), rwkv_g2=bf(wts["rwkv_g2"]),
        rwkv_k_k=row(wts["rwkv_k_k"]), rwkv_k_a=row(wts["rwkv_k_a"]), rwkv_r_k=row(wts["rwkv_r_k"]),
        rwkv_ln_w=row(wts["rwkv_ln_w"]), rwkv_ln_b=row(wts["rwkv_ln_b"]), w_out=bf(wts["w_out"]),
        norm_ffn_g=row(wts["norm_ffn_g"]), peer_w_q_t=bf(wts["peer_w_q"]).T,
        peer_sub_keys=bf(wts["peer_sub_keys"]).reshape(2 * PEER_HEADS, PEER_N_KEYS, PEER_N_KEYS),
        peer_u=bf(wts["peer_u"]), peer_v_t=bf(wts["peer_v"]).T,
        ple_norm_g=row(wts["ple_norm_g"]), ple_w_gate=bf(wts["ple_w_gate"]), ple_b_gate=row(wts["ple_b_gate"]),
        ple_w_proj=bf(wts["ple_w_proj"]),
    )


def _trunk(x, p, wts, final_g):
    bsz, seq, _ = x.shape
    depth = p.shape[0]
    t = bsz * seq
    consts = _prep_consts()
    consts["cos"], consts["sin"] = _rope_tables(seq)
    consts["scan_masks"] = _scan_masks()
    h = x.reshape(t, D_MODEL)
    fg = final_g.reshape(1, D_MODEL)
    for i in range(depth):
        lw = _layer_weights(i, wts)
        h = _mixer(h, bsz, seq, lw, consts)
        h = _peer(h, lw)
        h = _ple(h, p[i].reshape(t, PLE_DIM), lw["ple_norm_g"], lw["ple_w_gate"], lw["ple_b_gate"],
                 lw["ple_w_proj"], fg, final=(i == depth - 1))
    return h.reshape(bsz, seq, D_MODEL)


def kernel(x_prompt, x_sample, p_prompt, p_sample, norm_mix_g, w_in, att_norm_g, rwkv_shift, rwkv_w0, rwkv_w2, rwkv_a0, rwkv_a2, rwkv_g2, rwkv_k_k, rwkv_k_a, rwkv_r_k, rwkv_ln_w, rwkv_ln_b, w_out, norm_ffn_g, peer_w_q, peer_sub_keys, peer_u, peer_v, ple_norm_g, ple_w_gate, ple_b_gate, ple_w_proj, final_norm_g):
    wts = dict(norm_mix_g=norm_mix_g, w_in=w_in, att_norm_g=att_norm_g, rwkv_shift=rwkv_shift, rwkv_w0=rwkv_w0,
               rwkv_w2=rwkv_w2, rwkv_a0=rwkv_a0, rwkv_a2=rwkv_a2, rwkv_g2=rwkv_g2, rwkv_k_k=rwkv_k_k,
               rwkv_k_a=rwkv_k_a, rwkv_r_k=rwkv_r_k, rwkv_ln_w=rwkv_ln_w, rwkv_ln_b=rwkv_ln_b, w_out=w_out,
               norm_ffn_g=norm_ffn_g, peer_w_q=peer_w_q, peer_sub_keys=peer_sub_keys, peer_u=peer_u,
               peer_v=peer_v, ple_norm_g=ple_norm_g, ple_w_gate=ple_w_gate, ple_b_gate=ple_b_gate,
               ple_w_proj=ple_w_proj)
    nb = x_prompt.shape[0]
    x = jnp.concatenate([x_prompt, x_sample], axis=0)
    p = jnp.concatenate([p_prompt, p_sample], axis=1)
    y = _trunk(x, p, wts, final_norm_g)
    return (y[:nb], y[nb:])
```

```python
import functools

import jax
import jax.numpy as jnp
import numpy as np
from jax import lax
from jax.experimental import pallas as pl
from jax.experimental.pallas import tpu as pltpu

F32 = jnp.float32
BF16 = jnp.bfloat16

D_MODEL = 1024
HEAD_DIM = 64
ATT_WIDTH = 512
ATT_HEADS = 8
RWKV_WIDTH = 512
RWKV_COLS = 1920
QKV_COLS = 3 * ATT_WIDTH
DILATIONS = (1, 4, 16)
HALF_WINDOW = 64
ROPE_THETA = 500000.0
ROPE_DIM = 16
NORM_EPS = 1e-6
LN_X_EPS = 64e-5
NEG_INF = -1e30
PEER_HEADS = 8
PEER_N_KEYS = 128
PEER_TOPK = 16
PLE_DIM = 256
CHUNK = 64
GROUP = 256

VMEM_LIMIT = 56 * 1024 * 1024


def _cparams(*sem, flags=None):
    return pltpu.CompilerParams(dimension_semantics=sem, vmem_limit_bytes=VMEM_LIMIT, flags=flags)


def _full(shape):
    nd = len(shape)
    return pl.BlockSpec(shape, lambda *_: (0,) * nd)


def _dot(a, b):
    return jnp.dot(a, b, preferred_element_type=F32)


def _dot_nt(a, b):
    return lax.dot_general(a, b, (((1,), (1,)), ((), ())), preferred_element_type=F32)


def _dot_tn(a, b):
    return lax.dot_general(a, b, (((0,), (0,)), ((), ())), preferred_element_type=F32)


def _split3(x):
    hi = x.astype(BF16)
    r1 = x - hi.astype(F32)
    mid = r1.astype(BF16)
    lo = (r1 - mid.astype(F32)).astype(BF16)
    return hi, mid, lo


def _dot_lexact(l_bf16, x):
    hi, mid, lo = _split3(x)
    return _dot(l_bf16, hi) + _dot(l_bf16, mid) + _dot(l_bf16, lo)


def _dot_rexact(x, r_bf16):
    hi, mid, lo = _split3(x)
    return _dot(hi, r_bf16) + _dot(mid, r_bf16) + _dot(lo, r_bf16)


def _rms(x, g):
    return x * lax.rsqrt(jnp.mean(x * x, axis=-1, keepdims=True) + NORM_EPS) * g


def _sigmoid(x):
    return 1.0 / (1.0 + jnp.exp(-x))


LANES = 128
ATT_LANE_TILES = ATT_WIDTH // LANES


def _to_classes(t, stage, out_refs):
    rows = t.shape[0]
    for c in range(ATT_LANE_TILES):
        stage[c] = t[:, c * LANES:(c + 1) * LANES]
    for dil, ref in zip(DILATIONS, out_refs):
        if dil == 1:
            ref[...] = t.astype(ref.dtype)
            continue
        for r in range(dil):
            for c in range(ATT_LANE_TILES):
                lo = r * ATT_WIDTH + c * LANES
                ref[:, lo:lo + LANES] = stage[c, pl.ds(r, rows // dil, stride=dil), :].astype(ref.dtype)


def _from_classes(ref, dil, stage):
    if dil == 1:
        return ref[...]
    rows = ref.shape[0] * dil
    for r in range(dil):
        for c in range(ATT_LANE_TILES):
            lo = r * ATT_WIDTH + c * LANES
            stage[c, pl.ds(r, rows // dil, stride=dil), :] = ref[:, lo:lo + LANES]
    return jnp.concatenate([stage[c] for c in range(ATT_LANE_TILES)], axis=1)


def _inproj_kernel(h_ref, g_ref, w_ref, cos_ref, sin_ref, *refs):
    nd = len(DILATIONS)
    q_refs, k_refs, v_refs = refs[:nd], refs[nd:2 * nd], refs[2 * nd:3 * nd]
    u_ref, stage = refs[3 * nd:]
    xn = _rms(h_ref[...], g_ref[...]).astype(BF16)
    qkv = _dot(xn, w_ref[:, :QKV_COLS])
    u_ref[...] = _dot(xn, w_ref[:, QKV_COLS:])
    cos = cos_ref[...]
    sin = sin_ref[...]
    lane = lax.broadcasted_iota(jnp.int32, cos.shape, 1) % HEAD_DIM
    half = ROPE_DIM // 2

    def rot(t):
        partner = jnp.where(lane < half, pltpu.roll(t, ATT_WIDTH - half, 1), pltpu.roll(t, half, 1))
        return t * cos + partner * sin

    _to_classes(rot(qkv[:, :ATT_WIDTH]) * (HEAD_DIM ** -0.5), stage, q_refs)
    _to_classes(rot(qkv[:, ATT_WIDTH:2 * ATT_WIDTH]), stage, k_refs)
    _to_classes(qkv[:, 2 * ATT_WIDTH:], stage, v_refs)


def _inproj(h, g, w_bf16, cos_t, sin_t, seq, tm=512):
    t = h.shape[0]
    tm = min(tm, seq)
    nseq = seq // tm
    row = lambda i: (i, 0)
    pos = lambda i: (i % nseq, 0)
    cls_specs = [pl.BlockSpec((tm // d, d * ATT_WIDTH), row) for d in DILATIONS]
    cls_shapes = [jax.ShapeDtypeStruct((t // d, d * ATT_WIDTH), BF16) for d in DILATIONS]
    outs = pl.pallas_call(
        _inproj_kernel,
        grid=(t // tm,),
        in_specs=[pl.BlockSpec((tm, D_MODEL), row), _full((1, D_MODEL)), _full(w_bf16.shape),
                  pl.BlockSpec((tm, ATT_WIDTH), pos), pl.BlockSpec((tm, ATT_WIDTH), pos)],
        out_specs=cls_specs * 3 + [pl.BlockSpec((tm, RWKV_COLS), row)],
        out_shape=cls_shapes * 3 + [jax.ShapeDtypeStruct((t, RWKV_COLS), F32)],
        scratch_shapes=[pltpu.VMEM((ATT_LANE_TILES, tm, LANES), F32)],
        compiler_params=_cparams("parallel"),
        name="inproj",
    )(h, g, w_bf16, cos_t, sin_t)
    nd = len(DILATIONS)
    return outs[:nd], outs[nd:2 * nd], outs[2 * nd:3 * nd], outs[3 * nd]


def _rope_tables(seq):
    half = ROPE_DIM // 2
    inv_freq = ROPE_THETA ** (-jnp.arange(half, dtype=F32) / half)
    ang = jnp.arange(seq, dtype=F32)[:, None] * inv_freq[None, :]
    ones = jnp.ones((seq, HEAD_DIM - ROPE_DIM), F32)
    cos_h = jnp.concatenate([jnp.cos(ang), jnp.cos(ang), ones], axis=1)
    sin_h = jnp.concatenate([-jnp.sin(ang), jnp.sin(ang), 0.0 * ones], axis=1)
    return jnp.tile(cos_h, (1, ATT_HEADS)), jnp.tile(sin_h, (1, ATT_HEADS))


ATT_SUB = 128


def _attn_kernel(q_ref, kp_ref, kc_ref, kn_ref, vp_ref, vc_ref, vn_ref, o_ref, lse_ref,
                 kbuf, vbuf, *, length, tq):
    j = pl.program_id(2)
    hw = HALF_WINDOW
    kbuf[0:hw] = kp_ref[0]
    kbuf[hw:hw + tq] = kc_ref[0]
    kbuf[hw + tq:] = kn_ref[0]
    vbuf[0:hw] = vp_ref[0]
    vbuf[hw:hw + tq] = vc_ref[0]
    vbuf[hw + tq:] = vn_ref[0]
    nk = ATT_SUB + 2 * hw
    qi = lax.broadcasted_iota(jnp.int32, (ATT_SUB, nk), 0)
    ki = lax.broadcasted_iota(jnp.int32, (ATT_SUB, nk), 1)
    band = jnp.abs(ki - hw - qi) <= hw
    heads = [slice(h * HEAD_DIM, (h + 1) * HEAD_DIM) for h in range(ATT_HEADS)]
    blocks = [(s * ATT_SUB, c) for s in range(tq // ATT_SUB) for c in heads]
    sc = jnp.concatenate([_dot_nt(q_ref[0, q0:q0 + ATT_SUB, c], kbuf[q0:q0 + nk, c]) for q0, c in blocks], axis=0)
    valid = []
    for s in range(tq // ATT_SUB):
        kpos = j * tq + (s * ATT_SUB - hw) + ki
        valid += [band & (kpos >= 0) & (kpos < length)] * ATT_HEADS
    sc = jnp.where(jnp.concatenate(valid, axis=0), sc, NEG_INF)
    m = jnp.max(sc, axis=-1, keepdims=True)
    p = jnp.exp(sc - m)
    l = jnp.sum(p, axis=-1, keepdims=True)
    pb = p.astype(BF16)
    inv_l = 1.0 / l
    lse = m + jnp.log(l)
    for n, (q0, c) in enumerate(blocks):
        rows = slice(n * ATT_SUB, (n + 1) * ATT_SUB)
        o_ref[0, q0:q0 + ATT_SUB, c] = _dot(pb[rows], vbuf[q0:q0 + nk, c]) * inv_l[rows]
        lse_ref[0, q0:q0 + ATT_SUB, c] = jnp.broadcast_to(lse[rows], (ATT_SUB, HEAD_DIM))


def _attn_pattern(q, k, v, bsz, seq, dil):
    length = seq // dil
    tq = min(length, 512)
    nq = length // tq
    nh = length // HALF_WINDOW
    rq = tq // HALF_WINDOW
    view = lambda a: a.reshape(bsz, length, dil * ATT_WIDTH)
    cur = pl.BlockSpec((1, tq, ATT_WIDTH), lambda b, r, j: (b, j, r))
    prev = pl.BlockSpec((1, HALF_WINDOW, ATT_WIDTH), lambda b, r, j: (b, jnp.maximum(j * rq - 1, 0), r))
    nxt = pl.BlockSpec((1, HALF_WINDOW, ATT_WIDTH), lambda b, r, j: (b, jnp.minimum((j + 1) * rq, nh - 1), r))
    out_sd = jax.ShapeDtypeStruct((bsz, length, dil * ATT_WIDTH), F32)
    o, lse = pl.pallas_call(
        functools.partial(_attn_kernel, length=length, tq=tq),
        grid=(bsz, dil, nq),
        in_specs=[cur, prev, cur, nxt, prev, cur, nxt],
        out_specs=[cur, cur],
        out_shape=[out_sd, out_sd],
        scratch_shapes=[pltpu.VMEM((tq + 2 * HALF_WINDOW, ATT_WIDTH), BF16)] * 2,
        compiler_params=_cparams("parallel", "parallel", "parallel"),
        name=f"attn_d{dil}",
    )(view(q), view(k), view(k), view(k), view(v), view(v), view(v))
    flat = (bsz * length, dil * ATT_WIDTH)
    return o.reshape(flat), lse.reshape(flat)


R_TILE = 256
R_COLS = dict(r=0, k=512, v=1024, wd=1536, ad=1664, gd=1792)


def _rwkv_prep_kernel(uc_ref, up_ref, un_ref, shift_ref, w0_ref, w2_ref, a0_ref, a2_ref, g2_ref,
                      kk_ref, ka_ref, rk_ref, bd_ref, tri_ref, sel_ref,
                      rt_ref, at_ref, bt_ref, kt_ref, bh_ref, kh_ref, v_ref, wdec_ref, bonus_ref, g_ref):
    j = pl.program_id(1)
    nj = pl.num_programs(1)
    tr = R_TILE
    cur = uc_ref[0]
    row = lax.broadcasted_iota(jnp.int32, cur.shape, 0)
    prev_edge = up_ref[0, 7:8, :] * (j > 0).astype(F32)
    next_edge = un_ref[0, 0:1, :] * (j < nj - 1).astype(F32)
    prev = jnp.where(row == 0, prev_edge, pltpu.roll(cur, 1, 0))
    nxt = jnp.where(row == tr - 1, next_edge, pltpu.roll(cur, tr - 1, 0))
    us = shift_ref[0:1, :] * prev + shift_ref[1:2, :] * cur + shift_ref[2:3, :] * nxt

    w = RWKV_WIDTH
    r = us[:, R_COLS["r"]:R_COLS["r"] + w]
    k = us[:, R_COLS["k"]:R_COLS["k"] + w]
    v = us[:, R_COLS["v"]:R_COLS["v"] + w]
    gd = us[:, R_COLS["gd"]:]
    bd = bd_ref[...]

    g_ref[...] = _dot(_sigmoid(gd).astype(BF16), g2_ref[...])
    v_ref[...] = v.astype(BF16)
    kk = k * kk_ref[...]
    kk = kk / jnp.maximum(jnp.sqrt(_dot_rexact(kk * kk, bd)), 1e-12)

    nchunk = tr // CHUNK
    ksum = jnp.zeros_like(k)
    for d in range(2):
        wd = us[:, R_COLS["wd"] + 64 * d:R_COLS["wd"] + 64 * (d + 1)]
        ad = us[:, R_COLS["ad"] + 64 * d:R_COLS["ad"] + 64 * (d + 1)]
        z = w0_ref[d:d + 1, :] + _dot(jnp.tanh(wd).astype(BF16), w2_ref[d])
        w_log = -(jnp.maximum(-z, 0.0) + jnp.log(1.0 + jnp.exp(-jnp.abs(z)))) - 0.5
        lw = -jnp.exp(w_log)
        a = _sigmoid(a0_ref[d:d + 1, :] + _dot(ad.astype(BF16), a2_ref[d]))
        kd = k * (1.0 + (a - 1.0) * ka_ref[...])
        ksum = ksum + kd
        b = kk * a
        c = _dot_lexact(tri_ref[d], lw)
        tot = _dot_lexact(sel_ref[...], lw)
        cend = jnp.concatenate(
            [jnp.broadcast_to(tot[i:i + 1, :], (CHUNK, w)) for i in range(nchunk)], axis=0)
        e_c = jnp.exp(c)
        e_nc = jnp.exp(-c)
        e_rem = jnp.exp(cend - c)
        rt_ref[d] = (r * e_c).astype(BF16)
        at_ref[d] = (-kk * jnp.exp(c - lw)).astype(BF16)
        bt_ref[d] = (b * e_nc).astype(BF16)
        kt_ref[d] = (kd * e_nc).astype(BF16)
        bh_ref[d] = (b * e_rem).astype(BF16)
        kh_ref[d] = (kd * e_rem).astype(BF16)
        wdec = jnp.exp(tot)
        for i in range(nchunk):
            wdec_ref[d, 0, i] = wdec[i:i + 1, :]
    bonus_ref[...] = _dot_rexact(r * ksum * rk_ref[...], bd) * v


def _prep_consts():
    tr = R_TILE
    idx = np.arange(tr)
    same = (idx[:, None] // CHUNK) == (idx[None, :] // CHUNK)
    tri = np.stack([same & (idx[:, None] >= idx[None, :]), same & (idx[:, None] <= idx[None, :])])
    sel = (np.arange(8)[:, None] == (idx[None, :] // CHUNK))
    hd = np.arange(RWKV_WIDTH) // HEAD_DIM
    bd = hd[:, None] == hd[None, :]
    as_bf16 = lambda m: jnp.asarray(m.astype(np.float32), dtype=BF16)
    return dict(tri=as_bf16(tri), sel=as_bf16(sel), bd512=as_bf16(bd))


def _rwkv_prep(u, bsz, seq, shift_w, w0, w2, a0, a2, g2, k_k, k_a, r_k, consts):
    tr = R_TILE
    nj = seq // tr
    t = bsz * seq
    w = RWKV_WIDTH
    u3 = u.reshape(bsz, seq, RWKV_COLS)
    rows8 = tr // 8
    cur = pl.BlockSpec((1, tr, RWKV_COLS), lambda b, j: (b, j, 0))
    prev = pl.BlockSpec((1, 8, RWKV_COLS), lambda b, j: (b, jnp.maximum(j * rows8 - 1, 0), 0))
    nxt = pl.BlockSpec((1, 8, RWKV_COLS), lambda b, j: (b, jnp.minimum((j + 1) * rows8, seq // 8 - 1), 0))
    tok2 = pl.BlockSpec((2, tr, w), lambda b, j: (0, b * nj + j, 0))
    tok = pl.BlockSpec((tr, w), lambda b, j: (b * nj + j, 0))
    nchunk = tr // CHUNK
    wdec_spec = pl.BlockSpec((2, 1, nchunk, 1, w), lambda b, j: (0, b, j, 0, 0))
    small = [shift_w, w0, w2, a0, a2, g2, k_k, k_a, r_k, consts["bd512"], consts["tri"], consts["sel"]]
    sd2 = jax.ShapeDtypeStruct((2, t, w), BF16)
    return pl.pallas_call(
        _rwkv_prep_kernel,
        grid=(bsz, nj),
        in_specs=[cur, prev, nxt] + [_full(a.shape) for a in small],
        out_specs=[tok2] * 6 + [tok, wdec_spec, tok, tok],
        out_shape=[sd2] * 6 + [jax.ShapeDtypeStruct((t, w), BF16),
                               jax.ShapeDtypeStruct((2, bsz, seq // CHUNK, 1, w), F32),
                               jax.ShapeDtypeStruct((t, w), F32), jax.ShapeDtypeStruct((t, w), F32)],
        compiler_params=_cparams("parallel", "parallel"),
        name="rwkv_prep",
    )(u3, u3, u3, *small)


S_TILE = 256
LEVELS = (2, 4, 8, 16, 32)
N_DIR_OPERANDS = 6


def _rwkv_scan_kernel(*refs):
    nop = N_DIR_OPERANDS
    dir_refs = refs[:2 * nop]
    vf_ref, vb_ref, wf_ref, wb_ref, mask_ref, yf_ref, yb_ref, state = refs[2 * nop:]
    j = pl.program_id(2)

    @pl.when(j == 0)
    def _():
        state[...] = jnp.zeros_like(state)

    nchunk = S_TILE // CHUNK
    nh = GROUP // HEAD_DIM
    bdm = mask_ref[0]
    bdm_f32 = bdm.astype(F32)
    eye = mask_ref[1].astype(F32)
    tile4 = lambda x: jnp.concatenate([x] * nh, axis=0)
    io = ((vf_ref, wf_ref, yf_ref), (vb_ref, wb_ref, yb_ref))
    order = [(d, ci if d == 0 else nchunk - 1 - ci) for ci in range(nchunk) for d in range(2)]

    pre = {}
    for d, c in order:
        rt_ref, at_ref, bt_ref, kt_ref, bh_ref, kh_ref = [dir_refs[2 * i + d] for i in range(nop)]
        rows = slice(c * CHUNK, (c + 1) * CHUNK)
        a_bd = tile4(at_ref[0, rows, :]) * bdm
        r_bd = tile4(rt_ref[0, rows, :]) * bdm
        v_bd = tile4(io[d][0][0, rows, :]) * bdm
        ar = jnp.concatenate([a_bd, r_bd], axis=0)
        ar_b = _dot_nt(ar, tile4(bt_ref[0, rows, :]))
        ar_k = _dot_nt(ar, tile4(kt_ref[0, rows, :]))
        strict = mask_ref[2 + d]
        incl = mask_ref[4 + d]
        a_ab = ar_b[:GROUP].astype(BF16) * strict
        a_ak = ar_k[:GROUP].astype(BF16) * strict
        a_r = jnp.concatenate([ar_b[GROUP:].astype(BF16) * incl, ar_k[GROUP:].astype(BF16) * incl], axis=1)
        hat = jnp.concatenate([tile4(bh_ref[0, rows, :]), tile4(kh_ref[0, rows, :])], axis=0)
        pre[d, c] = dict(a_bd=a_bd, r_bd=r_bd, v_bd=v_bd, a_ab=a_ab, a_ak=a_ak, a_r=a_r, hat=hat,
                         tinv=eye + (a_ab * mask_ref[6 + d]).astype(F32))
    for li in range(len(LEVELS)):
        for d, c in order:
            p = pre[d, c]
            lm = p["a_ab"] * mask_ref[8 + 2 * li + d]
            tb = p["tinv"].astype(BF16)
            p["tinv"] = p["tinv"] + _dot(tb, _dot(lm, tb).astype(BF16))
    s = [state[0], state[1]]
    for d, c in order:
        p = pre[d, c]
        w_ref, y_ref = io[d][1], io[d][2]
        rows = slice(c * CHUNK, (c + 1) * CHUNK)
        s_bf = s[d].astype(BF16)
        rhs = _dot_nt(p["a_bd"], s_bf) + _dot(p["a_ak"], p["v_bd"])
        u_bf = _dot(p["tinv"].astype(BF16), rhs.astype(BF16)).astype(BF16)
        uv = jnp.concatenate([u_bf, p["v_bd"]], axis=0)
        y = _dot_nt(p["r_bd"], s_bf) + _dot(p["a_r"], uv)
        y_nat = y[0:CHUNK]
        for hh in range(1, nh):
            y_nat = y_nat + y[hh * CHUNK:(hh + 1) * CHUNK]
        y_ref[0, rows, :] = y_nat
        s[d] = s[d] * w_ref[0, 0, c] + _dot_tn(uv, p["hat"]) * bdm_f32
    state[0] = s[0]
    state[1] = s[1]


def _scan_masks():
    n = GROUP
    idx = np.arange(n)
    head = idx // CHUNK
    t = idx % CHUNK
    same = head[:, None] == head[None, :]
    tt, ss = t[:, None], t[None, :]
    masks = [same, np.eye(n, dtype=bool)]
    masks += [same & (tt > ss), same & (tt < ss)]
    masks += [same & (tt >= ss), same & (tt <= ss)]
    for m in (1,) + LEVELS:
        blk = same & (tt // (2 * m) == ss // (2 * m))
        lower = blk & ((tt // m) % 2 == 1) & ((ss // m) % 2 == 0)
        upper = blk & ((tt // m) % 2 == 0) & ((ss // m) % 2 == 1)
        masks += [lower, upper]
    return jnp.asarray(np.stack(masks).astype(np.float32), dtype=BF16)


def _rwkv_scan(dir_ops, v, wdec, bsz, seq, masks):
    st = S_TILE
    nj = seq // st
    ng = RWKV_WIDTH // GROUP
    w = RWKV_WIDTH
    t = bsz * seq
    nchunk = st // CHUNK
    blk = (1, st, GROUP)
    fwd = lambda d: pl.BlockSpec(blk, lambda b, g, j: (d, b * nj + j, g))
    bwd = lambda d: pl.BlockSpec(blk, lambda b, g, j: (d, b * nj + (nj - 1 - j), g))
    wf = pl.BlockSpec((1, 1, nchunk, 1, GROUP), lambda b, g, j: (0, b, j, 0, g))
    wb = pl.BlockSpec((1, 1, nchunk, 1, GROUP), lambda b, g, j: (1, b, nj - 1 - j, 0, g))
    v3 = v.reshape(1, t, w)
    in_specs, args = [], []
    for arr in dir_ops:
        in_specs += [fwd(0), bwd(1)]
        args += [arr, arr]
    in_specs += [fwd(0), bwd(0), wf, wb, _full(masks.shape)]
    args += [v3, v3, wdec, wdec, masks]
    sd = jax.ShapeDtypeStruct((1, t, w), F32)
    yf, yb = pl.pallas_call(
        _rwkv_scan_kernel,
        grid=(bsz, ng, nj),
        in_specs=in_specs,
        out_specs=[fwd(0), bwd(0)],
        out_shape=[sd, sd],
        scratch_shapes=[pltpu.VMEM((2, GROUP, GROUP), F32)],
        compiler_params=_cparams("parallel", "parallel", "arbitrary"),
        name="rwkv_scan",
    )(*args)
    return yf.reshape(t, w), yb.reshape(t, w)


def _outproj_kernel(h_ref, o1_ref, o2_ref, o3_ref, l1_ref, l2_ref, l3_ref, attg_ref,
                    yf_ref, yb_ref, bonus_ref, g_ref, lnw_ref, lnb_ref, bd_ref, w_ref, out_ref, stage):
    bd = bd_ref[...]
    inv_n = 1.0 / HEAD_DIM
    nd = len(DILATIONS)
    o1, o2, o3 = [_from_classes(r, d, stage.at[i])
                  for i, (r, d) in enumerate(zip((o1_ref, o2_ref, o3_ref), DILATIONS))]
    l1, l2, l3 = [_from_classes(r, d, stage.at[nd + i])
                  for i, (r, d) in enumerate(zip((l1_ref, l2_ref, l3_ref), DILATIONS))]
    m = jnp.maximum(jnp.maximum(l1, l2), l3)
    w1, w2, w3 = jnp.exp(l1 - m), jnp.exp(l2 - m), jnp.exp(l3 - m)
    o = (w1 * o1 + w2 * o2 + w3 * o3) / (w1 + w2 + w3)
    att = o * lax.rsqrt(_dot_rexact(o * o, bd) * inv_n + NORM_EPS) * attg_ref[...]
    y = yf_ref[...] + yb_ref[...]
    yc = y - _dot_rexact(y, bd) * inv_n
    var = _dot_rexact(yc * yc, bd) * inv_n
    yn = yc * lax.rsqrt(var + LN_X_EPS) * lnw_ref[...] + lnb_ref[...]
    rw = (yn + bonus_ref[...]) * g_ref[...]
    out_ref[...] = (h_ref[...] + _dot(att.astype(BF16), w_ref[:ATT_WIDTH, :])
                    + _dot(rw.astype(BF16), w_ref[ATT_WIDTH:, :]))


def _outproj(h, att_parts, att_g, yf, yb, bonus, g, ln_w, ln_b, bd512, w_out_bf16, tm=256):
    t = h.shape[0]
    row = lambda i: (i, 0)
    half = pl.BlockSpec((tm, ATT_WIDTH), row)
    cls = [pl.BlockSpec((tm // d, d * ATT_WIDTH), row) for d in DILATIONS]
    (o1, l1), (o2, l2), (o3, l3) = att_parts
    vec = _full((1, ATT_WIDTH))
    return pl.pallas_call(
        _outproj_kernel,
        grid=(t // tm,),
        in_specs=[pl.BlockSpec((tm, D_MODEL), row)] + cls * 2 + [vec] + [half] * 4 + [vec, vec,
                  _full(bd512.shape), _full(w_out_bf16.shape)],
        out_specs=pl.BlockSpec((tm, D_MODEL), row),
        out_shape=jax.ShapeDtypeStruct((t, D_MODEL), F32),
        scratch_shapes=[pltpu.VMEM((2 * len(DILATIONS), ATT_LANE_TILES, tm, LANES), F32)],
        compiler_params=_cparams("parallel"),
        name="outproj",
    )(h, o1, o2, o3, l1, l2, l3, att_g, yf, yb, bonus, g, ln_w, ln_b, bd512, w_out_bf16)


ROUTE_TILE = 256


def _top_rounds(cur, nrounds, on_round, track_rank):
    rank = jnp.full(cur.shape, float(nrounds), F32) if track_rank else None
    for r in range(nrounds):
        m = jnp.max(cur, axis=0, keepdims=True)
        on_round(r, m)
        hit = cur == m
        if track_rank:
            rank = jnp.where(hit, float(r), rank)
        if r + 1 < nrounds:
            cur = jnp.where(hit, NEG_INF, cur)
    return rank


def _peer_route_kernel(h_ref, g_ref, wq_ref, sk_ref, xn_ref, rank1_ref, e1_ref, cnt_ref, e0_ref, qt):
    tp = ROUTE_TILE
    nk = PEER_N_KEYS
    k = PEER_TOPK
    xn = _rms(h_ref[...], g_ref[...]).astype(BF16)
    xn_ref[...] = xn
    qt[...] = _dot_nt(wq_ref[...], xn)
    row16 = lax.broadcasted_iota(jnp.int32, (k, tp), 0)

    def one_head(h):
        tops, scores, ranks = [], [], []
        for p in range(2):
            hp = 2 * h + p
            q_hp = qt[pl.ds(pl.multiple_of(hp * nk, nk), nk), :].astype(BF16)
            s = _dot(sk_ref[hp], q_hp)
            scores.append(s)
            box = [jnp.full((k, tp), NEG_INF, F32)]

            def on_round(r, m, box=box):
                box[0] = jnp.where(row16 == r, m, box[0])

            ranks.append(_top_rounds(s, k, on_round, track_rank=(p == 1)))
            tops.append(box[0])
        t0, t1 = tops
        half = k // 2
        cand = jnp.concatenate([t0[0:1, :] + t1] + [t0[a:a + 1, :] + t1[0:half, :] for a in range(1, half)]
                               + [t0[half:, :] + t1[0:1, :]], axis=0)
        top = t0[0:1, :] + t1[0:1, :]
        acc = [jnp.zeros((1, tp), F32), top]

        def on_cand(r, m, acc=acc):
            acc[0] = acc[0] + jnp.exp(m - top)
            acc[1] = m

        _top_rounds(cand, k, on_cand, track_rank=False)
        z, thr = acc
        n = jnp.zeros((k, tp), F32)
        for b in range(k):
            n = n + jnp.where(t0 + t1[b:b + 1, :] >= thr, 1.0, 0.0)
        cnt = jnp.zeros((nk, tp), F32)
        for a in range(k):
            cnt = jnp.where(scores[0] == t0[a:a + 1, :], n[a:a + 1, :], cnt)
        rank1_ref[h] = ranks[1].astype(BF16)
        e1_ref[h] = jnp.exp(scores[1] - t1[0:1, :]).astype(BF16)
        cnt_ref[h] = cnt
        e0_ref[h] = jnp.exp(scores[0] - t0[0:1, :]) / z

    def head_pair(i, carry):
        one_head(2 * i)
        one_head(2 * i + 1)
        return carry

    lax.fori_loop(0, PEER_HEADS // 2, head_pair, 0)


def _peer_route(h, g, wq_t_bf16, sub_keys_bf16):
    t = h.shape[0]
    tp = ROUTE_TILE
    nk = PEER_N_KEYS
    nh = PEER_HEADS
    row = lambda i: (i, 0)
    lanes = lambda i: (0, 0, i)
    spec = pl.BlockSpec((nh, nk, tp), lanes)
    sd = lambda dt: jax.ShapeDtypeStruct((nh, nk, t), dt)
    return pl.pallas_call(
        _peer_route_kernel,
        grid=(t // tp,),
        in_specs=[pl.BlockSpec((tp, D_MODEL), row), _full((1, D_MODEL)), _full(wq_t_bf16.shape),
                  _full(sub_keys_bf16.shape)],
        out_specs=[pl.BlockSpec((tp, D_MODEL), row), spec, spec, spec, spec],
        out_shape=[jax.ShapeDtypeStruct((t, D_MODEL), BF16), sd(BF16), sd(BF16), sd(F32), sd(F32)],
        scratch_shapes=[pltpu.VMEM((nh * 2 * nk, tp), F32)],
        compiler_params=_cparams("parallel"),
        name="peer_route",
    )(h, g, wq_t_bf16, sub_keys_bf16)


E_TILE = 2048
DENSE_TILE = 512


BF16_ROWS = 16


def _rows_bf16(row, nrows):
    one = jnp.broadcast_to(row, (BF16_ROWS, row.shape[1])).astype(BF16)
    return jnp.concatenate([one] * (nrows // BF16_ROWS), axis=0)


def _peer_dense_kernel(xn_ref, u_ref, vt_ref, rank1_ref, e1_ref, cnt_ref, e0_ref, h_ref, out_ref, acc):
    e = pl.program_id(1)
    nk = PEER_N_KEYS
    tp = xn_ref.shape[0]

    @pl.when(e == 0)
    def _():
        acc[...] = jnp.zeros_like(acc)

    gates = []
    for ii in range(E_TILE // nk):
        i = e * (E_TILE // nk) + ii
        gate = jnp.zeros((nk, tp), BF16)
        for h in range(PEER_HEADS):
            cnt_row = _rows_bf16(cnt_ref[h, pl.ds(i, 1), :], nk)
            e0_row = _rows_bf16(e0_ref[h, pl.ds(i, 1), :], nk)
            gate = gate + jnp.where(rank1_ref[h] < cnt_row, e1_ref[h], jnp.zeros((), BF16)) * e0_row
        gates.append(gate)
    hid = _dot_nt(u_ref[...], xn_ref[...])
    hb = hid.astype(BF16)
    act = (0.5 * hb) * (1.0 + lax.erf(hb * (2.0 ** -0.5)))
    acc[...] += _dot(vt_ref[...], act * jnp.concatenate(gates, axis=0))

    @pl.when(e == pl.num_programs(1) - 1)
    def _():
        out_ref[...] = h_ref[...] + acc[...].T


def _peer_dense(h, xn, u_bf16, vt_bf16, rank1, e1, cnt, e0):
    t = h.shape[0]
    tp = min(DENSE_TILE, t)
    ne = u_bf16.shape[0]
    nh, nk = PEER_HEADS, PEER_N_KEYS
    tok = lambda i, e: (i, 0)
    lanes = lambda i, e: (0, 0, i)
    spec = pl.BlockSpec((nh, nk, tp), lanes)
    return pl.pallas_call(
        _peer_dense_kernel,
        grid=(t // tp, ne // E_TILE),
        in_specs=[pl.BlockSpec((tp, D_MODEL), tok), pl.BlockSpec((E_TILE, D_MODEL), lambda i, e: (e, 0)),
                  pl.BlockSpec((D_MODEL, E_TILE), lambda i, e: (0, e)), spec, spec, spec, spec,
                  pl.BlockSpec((tp, D_MODEL), tok)],
        out_specs=pl.BlockSpec((tp, D_MODEL), tok),
        out_shape=jax.ShapeDtypeStruct((t, D_MODEL), F32),
        scratch_shapes=[pltpu.VMEM((D_MODEL, tp), F32)],
        compiler_params=_cparams("parallel", "arbitrary"),
        name="peer_dense",
    )(xn, u_bf16, vt_bf16, rank1, e1, cnt, e0, h)


def _ple_kernel(h_ref, p_ref, g_ref, wg_ref, b_ref, wp_ref, fg_ref, out_ref, *, final):
    h = h_ref[...]
    gate = _sigmoid(_dot(_rms(h, g_ref[...]).astype(BF16), wg_ref[...]) + b_ref[...])
    out = h + gate * _dot(p_ref[...].astype(BF16), wp_ref[...])
    out_ref[...] = _rms(out, fg_ref[...]) if final else out


def _ple(h, p, g, wg_bf16, b, wp_bf16, final_g, final, tm=512):
    t = h.shape[0]
    row = lambda i: (i, 0)
    vec = _full((1, D_MODEL))
    return pl.pallas_call(
        functools.partial(_ple_kernel, final=final),
        grid=(t // tm,),
        in_specs=[pl.BlockSpec((tm, D_MODEL), row), pl.BlockSpec((tm, PLE_DIM), row), vec,
                  _full(wg_bf16.shape), vec, _full(wp_bf16.shape), vec],
        out_specs=pl.BlockSpec((tm, D_MODEL), row),
        out_shape=jax.ShapeDtypeStruct((t, D_MODEL), F32),
        compiler_params=_cparams("parallel"),
        name="ple_final" if final else "ple",
    )(h, p, g, wg_bf16, b, wp_bf16, final_g)


def _mixer(h, bsz, seq, lw, consts):
    qs, ks, vs, u = _inproj(h, lw["norm_mix_g"], lw["w_in"], consts["cos"], consts["sin"], seq)
    att_parts = [_attn_pattern(q, k, v, bsz, seq, dil) for q, k, v, dil in zip(qs, ks, vs, DILATIONS)]
    prep = _rwkv_prep(u, bsz, seq, lw["rwkv_shift"], lw["rwkv_w0"], lw["rwkv_w2"], lw["rwkv_a0"],
                      lw["rwkv_a2"], lw["rwkv_g2"], lw["rwkv_k_k"], lw["rwkv_k_a"], lw["rwkv_r_k"], consts)
    dir_ops, (rv, wdec, bonus, g) = prep[:N_DIR_OPERANDS], prep[N_DIR_OPERANDS:]
    yf, yb = _rwkv_scan(dir_ops, rv, wdec, bsz, seq, consts["scan_masks"])
    return _outproj(h, att_parts, lw["att_norm_g"], yf, yb, bonus, g, lw["rwkv_ln_w"], lw["rwkv_ln_b"],
                    consts["bd512"], lw["w_out"])


def _peer(h, lw):
    xn, rank1, e1, cnt, e0 = _peer_route(h, lw["norm_ffn_g"], lw["peer_w_q_t"], lw["peer_sub_keys"])
    return _peer_dense(h, xn, lw["peer_u"], lw["peer_v_t"], rank1, e1, cnt, e0)


def _layer_weights(i, wts):
    row = lambda a: a[i].reshape(1, -1)
    bf = lambda a: a[i].astype(BF16)
    return dict(
        norm_mix_g=row(wts["norm_mix_g"]), w_in=bf(wts["w_in"]), att_norm_g=row(wts["att_norm_g"]),
        rwkv_shift=wts["rwkv_shift"][i], rwkv_w0=wts["rwkv_w0"][i], rwkv_w2=bf(wts["rwkv_w2"]),
        rwkv_a0=wts["rwkv_a0"][i], rwkv_a2=bf(wts["rwkv_a2"]), rwkv_g2=bf(wts["rwkv_g2"]),
        rwkv_k_k=row(wts["rwkv_k_k"]), rwkv_k_a=row(wts["rwkv_k_a"]), rwkv_r_k=row(wts["rwkv_r_k"]),
        rwkv_ln_w=row(wts["rwkv_ln_w"]), rwkv_ln_b=row(wts["rwkv_ln_b"]), w_out=bf(wts["w_out"]),
        norm_ffn_g=row(wts["norm_ffn_g"]), peer_w_q_t=bf(wts["peer_w_q"]).T,
        peer_sub_keys=bf(wts["peer_sub_keys"]).reshape(2 * PEER_HEADS, PEER_N_KEYS, PEER_N_KEYS),
        peer_u=bf(wts["peer_u"]), peer_v_t=bf(wts["peer_v"]).T,
        ple_norm_g=row(wts["ple_norm_g"]), ple_w_gate=bf(wts["ple_w_gate"]), ple_b_gate=row(wts["ple_b_gate"]),
        ple_w_proj=bf(wts["ple_w_proj"]),
    )


def _trunk(x, p, wts, final_g):
    bsz, seq, _ = x.shape
    depth = p.shape[0]
    t = bsz * seq
    consts = _prep_consts()
    consts["cos"], consts["sin"] = _rope_tables(seq)
    consts["scan_masks"] = _scan_masks()
    h = x.reshape(t, D_MODEL)
    fg = final_g.reshape(1, D_MODEL)
    for i in range(depth):
        lw = _layer_weights(i, wts)
        h = _mixer(h, bsz, seq, lw, consts)
        h = _peer(h, lw)
        h = _ple(h, p[i].reshape(t, PLE_DIM), lw["ple_norm_g"], lw["ple_w_gate"], lw["ple_b_gate"],
                 lw["ple_w_proj"], fg, final=(i == depth - 1))
    return h.reshape(bsz, seq, D_MODEL)


def kernel(x_prompt, x_sample, p_prompt, p_sample, norm_mix_g, w_in, att_norm_g, rwkv_shift, rwkv_w0, rwkv_w2, rwkv_a0, rwkv_a2, rwkv_g2, rwkv_k_k, rwkv_k_a, rwkv_r_k, rwkv_ln_w, rwkv_ln_b, w_out, norm_ffn_g, peer_w_q, peer_sub_keys, peer_u, peer_v, ple_norm_g, ple_w_gate, ple_b_gate, ple_w_proj, final_norm_g):
    wts = dict(norm_mix_g=norm_mix_g, w_in=w_in, att_norm_g=att_norm_g, rwkv_shift=rwkv_shift, rwkv_w0=rwkv_w0,
               rwkv_w2=rwkv_w2, rwkv_a0=rwkv_a0, rwkv_a2=rwkv_a2, rwkv_g2=rwkv_g2, rwkv_k_k=rwkv_k_k,
               rwkv_k_a=rwkv_k_a, rwkv_r_k=rwkv_r_k, rwkv_ln_w=rwkv_ln_w, rwkv_ln_b=rwkv_ln_b, w_out=w_out,
               norm_ffn_g=norm_ffn_g, peer_w_q=peer_w_q, peer_sub_keys=peer_sub_keys, peer_u=peer_u,
               peer_v=peer_v, ple_norm_g=ple_norm_g, ple_w_gate=ple_w_gate, ple_b_gate=ple_b_gate,
               ple_w_proj=ple_w_proj)
    nb = x_prompt.shape[0]
    x = jnp.concatenate([x_prompt, x_sample], axis=0)
    p = jnp.concatenate([p_prompt, p_sample], axis=1)
    y = _trunk(x, p, wts, final_norm_g)
    return (y[:nb], y[nb:])
```

```python
import functools

import jax
import jax.numpy as jnp
import numpy as np
from jax import lax
from jax.experimental import pallas as pl
from jax.experimental.pallas import tpu as pltpu

F32 = jnp.float32
BF16 = jnp.bfloat16

D_MODEL = 1024
HEAD_DIM = 64
ATT_WIDTH = 512
ATT_HEADS = 8
RWKV_WIDTH = 512
RWKV_COLS = 1920
QKV_COLS = 3 * ATT_WIDTH
DILATIONS = (1, 4, 16)
HALF_WINDOW = 64
ROPE_THETA = 500000.0
ROPE_DIM = 16
NORM_EPS = 1e-6
LN_X_EPS = 64e-5
NEG_INF = -1e30
PEER_HEADS = 8
PEER_N_KEYS = 128
PEER_TOPK = 16
PLE_DIM = 256
CHUNK = 64
GROUP = 256

VMEM_LIMIT = 56 * 1024 * 1024


def _cparams(*sem, flags=None):
    return pltpu.CompilerParams(dimension_semantics=sem, vmem_limit_bytes=VMEM_LIMIT, flags=flags)


def _full(shape):
    nd = len(shape)
    return pl.BlockSpec(shape, lambda *_: (0,) * nd)


def _dot(a, b):
    return jnp.dot(a, b, preferred_element_type=F32)


def _dot_nt(a, b):
    return lax.dot_general(a, b, (((1,), (1,)), ((), ())), preferred_element_type=F32)


def _dot_tn(a, b):
    return lax.dot_general(a, b, (((0,), (0,)), ((), ())), preferred_element_type=F32)


def _split3(x):
    hi = x.astype(BF16)
    r1 = x - hi.astype(F32)
    mid = r1.astype(BF16)
    lo = (r1 - mid.astype(F32)).astype(BF16)
    return hi, mid, lo


def _dot_rexact(x, r_bf16):
    hi, mid, lo = _split3(x)
    return _dot(hi, r_bf16) + _dot(mid, r_bf16) + _dot(lo, r_bf16)


def _rms(x, g):
    return x * lax.rsqrt(jnp.mean(x * x, axis=-1, keepdims=True) + NORM_EPS) * g


def _sigmoid(x):
    return 1.0 / (1.0 + jnp.exp(-x))


LANES = 128
ATT_LANE_TILES = ATT_WIDTH // LANES


def _to_classes(t, stage, out_refs):
    rows = t.shape[0]
    for c in range(ATT_LANE_TILES):
        stage[c] = t[:, c * LANES:(c + 1) * LANES]
    for dil, ref in zip(DILATIONS, out_refs):
        if dil == 1:
            ref[...] = t.astype(ref.dtype)
            continue
        for r in range(dil):
            for c in range(ATT_LANE_TILES):
                lo = r * ATT_WIDTH + c * LANES
                ref[:, lo:lo + LANES] = stage[c, pl.ds(r, rows // dil, stride=dil), :].astype(ref.dtype)


def _from_classes(ref, dil, stage):
    if dil == 1:
        return ref[...]
    rows = ref.shape[0] * dil
    for r in range(dil):
        for c in range(ATT_LANE_TILES):
            lo = r * ATT_WIDTH + c * LANES
            stage[c, pl.ds(r, rows // dil, stride=dil), :] = ref[:, lo:lo + LANES]
    return jnp.concatenate([stage[c] for c in range(ATT_LANE_TILES)], axis=1)


def _inproj_kernel(h_ref, g_ref, w_ref, cos_ref, sin_ref, *refs):
    nd = len(DILATIONS)
    q_refs, k_refs, v_refs = refs[:nd], refs[nd:2 * nd], refs[2 * nd:3 * nd]
    u_ref, stage = refs[3 * nd:]
    xn = _rms(h_ref[...], g_ref[...]).astype(BF16)
    qkv = _dot(xn, w_ref[:, :QKV_COLS])
    u_ref[...] = _dot(xn, w_ref[:, QKV_COLS:])
    cos = cos_ref[...]
    sin = sin_ref[...]
    lane = lax.broadcasted_iota(jnp.int32, cos.shape, 1) % HEAD_DIM
    half = ROPE_DIM // 2

    def rot(t):
        partner = jnp.where(lane < half, pltpu.roll(t, ATT_WIDTH - half, 1), pltpu.roll(t, half, 1))
        return t * cos + partner * sin

    _to_classes(rot(qkv[:, :ATT_WIDTH]) * (HEAD_DIM ** -0.5), stage, q_refs)
    _to_classes(rot(qkv[:, ATT_WIDTH:2 * ATT_WIDTH]), stage, k_refs)
    _to_classes(qkv[:, 2 * ATT_WIDTH:], stage, v_refs)


def _inproj(h, g, w_bf16, cos_t, sin_t, seq, tm=512):
    t = h.shape[0]
    tm = min(tm, seq)
    nseq = seq // tm
    row = lambda i: (i, 0)
    pos = lambda i: (i % nseq, 0)
    cls_specs = [pl.BlockSpec((tm // d, d * ATT_WIDTH), row) for d in DILATIONS]
    cls_shapes = [jax.ShapeDtypeStruct((t // d, d * ATT_WIDTH), BF16) for d in DILATIONS]
    outs = pl.pallas_call(
        _inproj_kernel,
        grid=(t // tm,),
        in_specs=[pl.BlockSpec((tm, D_MODEL), row), _full((1, D_MODEL)), _full(w_bf16.shape),
                  pl.BlockSpec((tm, ATT_WIDTH), pos), pl.BlockSpec((tm, ATT_WIDTH), pos)],
        out_specs=cls_specs * 3 + [pl.BlockSpec((tm, RWKV_COLS), row)],
        out_shape=cls_shapes * 3 + [jax.ShapeDtypeStruct((t, RWKV_COLS), F32)],
        scratch_shapes=[pltpu.VMEM((ATT_LANE_TILES, tm, LANES), F32)],
        compiler_params=_cparams("parallel"),
        name="inproj",
    )(h, g, w_bf16, cos_t, sin_t)
    nd = len(DILATIONS)
    return outs[:nd], outs[nd:2 * nd], outs[2 * nd:3 * nd], outs[3 * nd]


def _rope_tables(seq):
    half = ROPE_DIM // 2
    inv_freq = ROPE_THETA ** (-jnp.arange(half, dtype=F32) / half)
    ang = jnp.arange(seq, dtype=F32)[:, None] * inv_freq[None, :]
    ones = jnp.ones((seq, HEAD_DIM - ROPE_DIM), F32)
    cos_h = jnp.concatenate([jnp.cos(ang), jnp.cos(ang), ones], axis=1)
    sin_h = jnp.concatenate([-jnp.sin(ang), jnp.sin(ang), 0.0 * ones], axis=1)
    return jnp.tile(cos_h, (1, ATT_HEADS)), jnp.tile(sin_h, (1, ATT_HEADS))


ATT_SUB = 128


def _attn_kernel(q_ref, kp_ref, kc_ref, kn_ref, vp_ref, vc_ref, vn_ref, o_ref, lse_ref,
                 kbuf, vbuf, *, length, tq):
    j = pl.program_id(2)
    hw = HALF_WINDOW
    kbuf[0:hw] = kp_ref[0]
    kbuf[hw:hw + tq] = kc_ref[0]
    kbuf[hw + tq:] = kn_ref[0]
    vbuf[0:hw] = vp_ref[0]
    vbuf[hw:hw + tq] = vc_ref[0]
    vbuf[hw + tq:] = vn_ref[0]
    nk = ATT_SUB + 2 * hw
    qi = lax.broadcasted_iota(jnp.int32, (ATT_SUB, nk), 0)
    ki = lax.broadcasted_iota(jnp.int32, (ATT_SUB, nk), 1)
    band = jnp.abs(ki - hw - qi) <= hw
    heads = [slice(h * HEAD_DIM, (h + 1) * HEAD_DIM) for h in range(ATT_HEADS)]
    blocks = [(s * ATT_SUB, c) for s in range(tq // ATT_SUB) for c in heads]
    sc = jnp.concatenate([_dot_nt(q_ref[0, q0:q0 + ATT_SUB, c], kbuf[q0:q0 + nk, c]) for q0, c in blocks], axis=0)
    valid = []
    for s in range(tq // ATT_SUB):
        kpos = j * tq + (s * ATT_SUB - hw) + ki
        valid += [band & (kpos >= 0) & (kpos < length)] * ATT_HEADS
    sc = jnp.where(jnp.concatenate(valid, axis=0), sc, NEG_INF)
    m = jnp.max(sc, axis=-1, keepdims=True)
    p = jnp.exp(sc - m)
    l = jnp.sum(p, axis=-1, keepdims=True)
    pb = p.astype(BF16)
    inv_l = 1.0 / l
    lse = m + jnp.log(l)
    for n, (q0, c) in enumerate(blocks):
        rows = slice(n * ATT_SUB, (n + 1) * ATT_SUB)
        o_ref[0, q0:q0 + ATT_SUB, c] = _dot(pb[rows], vbuf[q0:q0 + nk, c]) * inv_l[rows]
        lse_ref[0, q0:q0 + ATT_SUB, c] = jnp.broadcast_to(lse[rows], (ATT_SUB, HEAD_DIM))


def _attn_pattern(q, k, v, bsz, seq, dil):
    length = seq // dil
    tq = min(length, 512)
    nq = length // tq
    nh = length // HALF_WINDOW
    rq = tq // HALF_WINDOW
    view = lambda a: a.reshape(bsz, length, dil * ATT_WIDTH)
    cur = pl.BlockSpec((1, tq, ATT_WIDTH), lambda b, r, j: (b, j, r))
    prev = pl.BlockSpec((1, HALF_WINDOW, ATT_WIDTH), lambda b, r, j: (b, jnp.maximum(j * rq - 1, 0), r))
    nxt = pl.BlockSpec((1, HALF_WINDOW, ATT_WIDTH), lambda b, r, j: (b, jnp.minimum((j + 1) * rq, nh - 1), r))
    out_sd = jax.ShapeDtypeStruct((bsz, length, dil * ATT_WIDTH), F32)
    o, lse = pl.pallas_call(
        functools.partial(_attn_kernel, length=length, tq=tq),
        grid=(bsz, dil, nq),
        in_specs=[cur, prev, cur, nxt, prev, cur, nxt],
        out_specs=[cur, cur],
        out_shape=[out_sd, out_sd],
        scratch_shapes=[pltpu.VMEM((tq + 2 * HALF_WINDOW, ATT_WIDTH), BF16)] * 2,
        compiler_params=_cparams("parallel", "parallel", "parallel"),
        name=f"attn_d{dil}",
    )(view(q), view(k), view(k), view(k), view(v), view(v), view(v))
    flat = (bsz * length, dil * ATT_WIDTH)
    return o.reshape(flat), lse.reshape(flat)


R_TILE = 256
R_COLS = dict(r=0, k=512, v=1024, wd=1536, ad=1664, gd=1792)
DECAY_LOG_SCALE = float(np.exp(-0.5))


def _rwkv_prep_kernel(uc_ref, up_ref, un_ref, shift_ref, w0_ref, w2_ref, a0_ref, a2_ref, g2_ref,
                      kk_ref, ka_ref, rk_ref, bd_ref, tri_ref, sel_ref,
                      rt_ref, at_ref, bt_ref, kt_ref, bh_ref, kh_ref, v_ref, wdec_ref, bonus_ref, g_ref):
    j = pl.program_id(1)
    nj = pl.num_programs(1)
    tr = R_TILE
    cur = uc_ref[0]
    row = lax.broadcasted_iota(jnp.int32, cur.shape, 0)
    prev_edge = up_ref[0, 7:8, :] * (j > 0).astype(F32)
    next_edge = un_ref[0, 0:1, :] * (j < nj - 1).astype(F32)
    prev = jnp.where(row == 0, prev_edge, pltpu.roll(cur, 1, 0))
    nxt = jnp.where(row == tr - 1, next_edge, pltpu.roll(cur, tr - 1, 0))
    us = shift_ref[0:1, :] * prev + shift_ref[1:2, :] * cur + shift_ref[2:3, :] * nxt

    w = RWKV_WIDTH
    r = us[:, R_COLS["r"]:R_COLS["r"] + w]
    k = us[:, R_COLS["k"]:R_COLS["k"] + w]
    v = us[:, R_COLS["v"]:R_COLS["v"] + w]
    gd = us[:, R_COLS["gd"]:]
    bd = bd_ref[...]

    g_ref[...] = _dot(_sigmoid(gd).astype(BF16), g2_ref[...])
    v_ref[...] = v.astype(BF16)
    kk = k * kk_ref[...]
    kk = kk / jnp.maximum(jnp.sqrt(_dot_rexact(kk * kk, bd)), 1e-12)

    nchunk = tr // CHUNK
    ksum = jnp.zeros_like(k)
    for d in range(2):
        wd = us[:, R_COLS["wd"] + 64 * d:R_COLS["wd"] + 64 * (d + 1)]
        ad = us[:, R_COLS["ad"] + 64 * d:R_COLS["ad"] + 64 * (d + 1)]
        z = w0_ref[d:d + 1, :] + _dot(jnp.tanh(wd).astype(BF16), w2_ref[d])
        lw = -DECAY_LOG_SCALE * _sigmoid(z)
        a = _sigmoid(a0_ref[d:d + 1, :] + _dot(ad.astype(BF16), a2_ref[d]))
        kd = k * (1.0 + (a - 1.0) * ka_ref[...])
        ksum = ksum + kd
        b = kk * a
        lw3 = _split3(lw)
        sum3 = lambda l: _dot(l, lw3[0]) + _dot(l, lw3[1]) + _dot(l, lw3[2])
        c = sum3(tri_ref[d])
        tot = sum3(sel_ref[...])
        cend = jnp.concatenate(
            [jnp.broadcast_to(tot[i:i + 1, :], (CHUNK, w)) for i in range(nchunk)], axis=0)
        e_c = jnp.exp(c)
        e_nc = jnp.exp(-c)
        e_rem = jnp.exp(cend - c)
        rt_ref[d] = (r * e_c).astype(BF16)
        at_ref[d] = (-kk * jnp.exp(c - lw)).astype(BF16)
        bt_ref[d] = (b * e_nc).astype(BF16)
        kt_ref[d] = (kd * e_nc).astype(BF16)
        bh_ref[d] = (b * e_rem).astype(BF16)
        kh_ref[d] = (kd * e_rem).astype(BF16)
        wdec = jnp.exp(tot)
        for i in range(nchunk):
            wdec_ref[d, 0, i] = wdec[i:i + 1, :]
    bonus_ref[...] = _dot_rexact(r * ksum * rk_ref[...], bd) * v


def _prep_consts():
    tr = R_TILE
    idx = np.arange(tr)
    same = (idx[:, None] // CHUNK) == (idx[None, :] // CHUNK)
    tri = np.stack([same & (idx[:, None] >= idx[None, :]), same & (idx[:, None] <= idx[None, :])])
    sel = (np.arange(8)[:, None] == (idx[None, :] // CHUNK))
    hd = np.arange(RWKV_WIDTH) // HEAD_DIM
    bd = hd[:, None] == hd[None, :]
    as_bf16 = lambda m: jnp.asarray(m.astype(np.float32), dtype=BF16)
    return dict(tri=as_bf16(tri), sel=as_bf16(sel), bd512=as_bf16(bd))


def _rwkv_prep(u, bsz, seq, shift_w, w0, w2, a0, a2, g2, k_k, k_a, r_k, consts):
    tr = R_TILE
    nj = seq // tr
    t = bsz * seq
    w = RWKV_WIDTH
    u3 = u.reshape(bsz, seq, RWKV_COLS)
    rows8 = tr // 8
    cur = pl.BlockSpec((1, tr, RWKV_COLS), lambda b, j: (b, j, 0))
    prev = pl.BlockSpec((1, 8, RWKV_COLS), lambda b, j: (b, jnp.maximum(j * rows8 - 1, 0), 0))
    nxt = pl.BlockSpec((1, 8, RWKV_COLS), lambda b, j: (b, jnp.minimum((j + 1) * rows8, seq // 8 - 1), 0))
    tok2 = pl.BlockSpec((2, tr, w), lambda b, j: (0, b * nj + j, 0))
    tok = pl.BlockSpec((tr, w), lambda b, j: (b * nj + j, 0))
    nchunk = tr // CHUNK
    wdec_spec = pl.BlockSpec((2, 1, nchunk, 1, w), lambda b, j: (0, b, j, 0, 0))
    small = [shift_w, w0, w2, a0, a2, g2, k_k, k_a, r_k, consts["bd512"], consts["tri"], consts["sel"]]
    sd2 = jax.ShapeDtypeStruct((2, t, w), BF16)
    return pl.pallas_call(
        _rwkv_prep_kernel,
        grid=(bsz, nj),
        in_specs=[cur, prev, nxt] + [_full(a.shape) for a in small],
        out_specs=[tok2] * 6 + [tok, wdec_spec, tok, tok],
        out_shape=[sd2] * 6 + [jax.ShapeDtypeStruct((t, w), BF16),
                               jax.ShapeDtypeStruct((2, bsz, seq // CHUNK, 1, w), F32),
                               jax.ShapeDtypeStruct((t, w), F32), jax.ShapeDtypeStruct((t, w), F32)],
        compiler_params=_cparams("parallel", "parallel"),
        name="rwkv_prep",
    )(u3, u3, u3, *small)


S_TILE = 256
LEVELS = (2, 4, 8, 16, 32)
N_DIR_OPERANDS = 6


def _rwkv_scan_kernel(*refs):
    nop = N_DIR_OPERANDS
    dir_refs = refs[:2 * nop]
    vf_ref, vb_ref, wf_ref, wb_ref, mask_ref, yf_ref, yb_ref, state = refs[2 * nop:]
    j = pl.program_id(2)

    @pl.when(j == 0)
    def _():
        state[...] = jnp.zeros_like(state)

    nchunk = S_TILE // CHUNK
    nh = GROUP // HEAD_DIM
    bdm = mask_ref[0]
    bdm_f32 = bdm.astype(F32)
    eye = mask_ref[1].astype(F32)
    tile4 = lambda x: jnp.concatenate([x] * nh, axis=0)
    io = ((vf_ref, wf_ref, yf_ref), (vb_ref, wb_ref, yb_ref))
    order = [(d, ci if d == 0 else nchunk - 1 - ci) for ci in range(nchunk) for d in range(2)]

    pre = {}
    for d, c in order:
        rt_ref, at_ref, bt_ref, kt_ref, bh_ref, kh_ref = [dir_refs[2 * i + d] for i in range(nop)]
        rows = slice(c * CHUNK, (c + 1) * CHUNK)
        a_bd = tile4(at_ref[0, rows, :]) * bdm
        r_bd = tile4(rt_ref[0, rows, :]) * bdm
        v_bd = tile4(io[d][0][0, rows, :]) * bdm
        ar = jnp.concatenate([a_bd, r_bd], axis=0)
        ar_b = _dot_nt(ar, tile4(bt_ref[0, rows, :]))
        ar_k = _dot_nt(ar, tile4(kt_ref[0, rows, :]))
        strict = mask_ref[2 + d]
        incl = mask_ref[4 + d]
        a_ab = ar_b[:GROUP].astype(BF16) * strict
        a_ak = ar_k[:GROUP].astype(BF16) * strict
        a_r = jnp.concatenate([ar_b[GROUP:].astype(BF16) * incl, ar_k[GROUP:].astype(BF16) * incl], axis=1)
        hat = jnp.concatenate([tile4(bh_ref[0, rows, :]), tile4(kh_ref[0, rows, :])], axis=0)
        pre[d, c] = dict(a_bd=a_bd, r_bd=r_bd, v_bd=v_bd, a_ab=a_ab, a_ak=a_ak, a_r=a_r, hat=hat,
                         tinv=eye + (a_ab * mask_ref[6 + d]).astype(F32))
    for li in range(len(LEVELS)):
        for d, c in order:
            p = pre[d, c]
            lm = p["a_ab"] * mask_ref[8 + 2 * li + d]
            tb = p["tinv"].astype(BF16)
            p["tinv"] = p["tinv"] + _dot(tb, _dot(lm, tb).astype(BF16))
    s = [state[0], state[1]]
    for d, c in order:
        p = pre[d, c]
        w_ref, y_ref = io[d][1], io[d][2]
        rows = slice(c * CHUNK, (c + 1) * CHUNK)
        s_bf = s[d].astype(BF16)
        rhs = _dot_nt(p["a_bd"], s_bf) + _dot(p["a_ak"], p["v_bd"])
        u_bf = _dot(p["tinv"].astype(BF16), rhs.astype(BF16)).astype(BF16)
        uv = jnp.concatenate([u_bf, p["v_bd"]], axis=0)
        y = _dot_nt(p["r_bd"], s_bf) + _dot(p["a_r"], uv)
        y_nat = y[0:CHUNK]
        for hh in range(1, nh):
            y_nat = y_nat + y[hh * CHUNK:(hh + 1) * CHUNK]
        y_ref[0, rows, :] = y_nat
        s[d] = s[d] * w_ref[0, 0, c] + _dot_tn(uv, p["hat"]) * bdm_f32
    state[0] = s[0]
    state[1] = s[1]


def _scan_masks():
    n = GROUP
    idx = np.arange(n)
    head = idx // CHUNK
    t = idx % CHUNK
    same = head[:, None] == head[None, :]
    tt, ss = t[:, None], t[None, :]
    masks = [same, np.eye(n, dtype=bool)]
    masks += [same & (tt > ss), same & (tt < ss)]
    masks += [same & (tt >= ss), same & (tt <= ss)]
    for m in (1,) + LEVELS:
        blk = same & (tt // (2 * m) == ss // (2 * m))
        lower = blk & ((tt // m) % 2 == 1) & ((ss // m) % 2 == 0)
        upper = blk & ((tt // m) % 2 == 0) & ((ss // m) % 2 == 1)
        masks += [lower, upper]
    return jnp.asarray(np.stack(masks).astype(np.float32), dtype=BF16)


def _rwkv_scan(dir_ops, v, wdec, bsz, seq, masks):
    st = S_TILE
    nj = seq // st
    ng = RWKV_WIDTH // GROUP
    w = RWKV_WIDTH
    t = bsz * seq
    nchunk = st // CHUNK
    blk = (1, st, GROUP)
    fwd = lambda d: pl.BlockSpec(blk, lambda b, g, j: (d, b * nj + j, g))
    bwd = lambda d: pl.BlockSpec(blk, lambda b, g, j: (d, b * nj + (nj - 1 - j), g))
    wf = pl.BlockSpec((1, 1, nchunk, 1, GROUP), lambda b, g, j: (0, b, j, 0, g))
    wb = pl.BlockSpec((1, 1, nchunk, 1, GROUP), lambda b, g, j: (1, b, nj - 1 - j, 0, g))
    v3 = v.reshape(1, t, w)
    in_specs, args = [], []
    for arr in dir_ops:
        in_specs += [fwd(0), bwd(1)]
        args += [arr, arr]
    in_specs += [fwd(0), bwd(0), wf, wb, _full(masks.shape)]
    args += [v3, v3, wdec, wdec, masks]
    sd = jax.ShapeDtypeStruct((1, t, w), F32)
    yf, yb = pl.pallas_call(
        _rwkv_scan_kernel,
        grid=(bsz, ng, nj),
        in_specs=in_specs,
        out_specs=[fwd(0), bwd(0)],
        out_shape=[sd, sd],
        scratch_shapes=[pltpu.VMEM((2, GROUP, GROUP), F32)],
        compiler_params=_cparams("parallel", "parallel", "arbitrary"),
        name="rwkv_scan",
    )(*args)
    return yf.reshape(t, w), yb.reshape(t, w)


def _outproj_kernel(h_ref, o1_ref, o2_ref, o3_ref, l1_ref, l2_ref, l3_ref, attg_ref,
                    yf_ref, yb_ref, bonus_ref, g_ref, lnw_ref, lnb_ref, bd_ref, w_ref, out_ref, stage):
    bd = bd_ref[...]
    inv_n = 1.0 / HEAD_DIM
    nd = len(DILATIONS)
    o1, o2, o3 = [_from_classes(r, d, stage.at[i])
                  for i, (r, d) in enumerate(zip((o1_ref, o2_ref, o3_ref), DILATIONS))]
    l1, l2, l3 = [_from_classes(r, d, stage.at[nd + i])
                  for i, (r, d) in enumerate(zip((l1_ref, l2_ref, l3_ref), DILATIONS))]
    m = jnp.maximum(jnp.maximum(l1, l2), l3)
    w1, w2, w3 = jnp.exp(l1 - m), jnp.exp(l2 - m), jnp.exp(l3 - m)
    o = (w1 * o1 + w2 * o2 + w3 * o3) / (w1 + w2 + w3)
    att = o * lax.rsqrt(_dot_rexact(o * o, bd) * inv_n + NORM_EPS) * attg_ref[...]
    y = yf_ref[...] + yb_ref[...]
    yc = y - _dot_rexact(y, bd) * inv_n
    var = _dot_rexact(yc * yc, bd) * inv_n
    yn = yc * lax.rsqrt(var + LN_X_EPS) * lnw_ref[...] + lnb_ref[...]
    rw = (yn + bonus_ref[...]) * g_ref[...]
    out_ref[...] = (h_ref[...] + _dot(att.astype(BF16), w_ref[:ATT_WIDTH, :])
                    + _dot(rw.astype(BF16), w_ref[ATT_WIDTH:, :]))


def _outproj(h, att_parts, att_g, yf, yb, bonus, g, ln_w, ln_b, bd512, w_out_bf16, tm=256):
    t = h.shape[0]
    row = lambda i: (i, 0)
    half = pl.BlockSpec((tm, ATT_WIDTH), row)
    cls = [pl.BlockSpec((tm // d, d * ATT_WIDTH), row) for d in DILATIONS]
    (o1, l1), (o2, l2), (o3, l3) = att_parts
    vec = _full((1, ATT_WIDTH))
    return pl.pallas_call(
        _outproj_kernel,
        grid=(t // tm,),
        in_specs=[pl.BlockSpec((tm, D_MODEL), row)] + cls * 2 + [vec] + [half] * 4 + [vec, vec,
                  _full(bd512.shape), _full(w_out_bf16.shape)],
        out_specs=pl.BlockSpec((tm, D_MODEL), row),
        out_shape=jax.ShapeDtypeStruct((t, D_MODEL), F32),
        scratch_shapes=[pltpu.VMEM((2 * len(DILATIONS), ATT_LANE_TILES, tm, LANES), F32)],
        compiler_params=_cparams("parallel"),
        name="outproj",
    )(h, o1, o2, o3, l1, l2, l3, att_g, yf, yb, bonus, g, ln_w, ln_b, bd512, w_out_bf16)


ROUTE_TILE = 256


def _top_rounds(cur, nrounds, on_round, track_rank):
    rank = jnp.full(cur.shape, float(nrounds), F32) if track_rank else None
    for r in range(nrounds):
        m = jnp.max(cur, axis=0, keepdims=True)
        on_round(r, m)
        hit = cur == m
        if track_rank:
            rank = jnp.where(hit, float(r), rank)
        if r + 1 < nrounds:
            cur = jnp.where(hit, NEG_INF, cur)
    return rank


def _peer_route_kernel(h_ref, g_ref, wq_ref, sk_ref, xn_ref, rank1_ref, e1_ref, cnt_ref, e0_ref, qt):
    tp = ROUTE_TILE
    nk = PEER_N_KEYS
    k = PEER_TOPK
    xn = _rms(h_ref[...], g_ref[...]).astype(BF16)
    xn_ref[...] = xn
    qt[...] = _dot_nt(wq_ref[...], xn)
    row16 = lax.broadcasted_iota(jnp.int32, (k, tp), 0)

    def one_head(h):
        tops, scores, ranks = [], [], []
        for p in range(2):
            hp = 2 * h + p
            q_hp = qt[pl.ds(pl.multiple_of(hp * nk, nk), nk), :].astype(BF16)
            s = _dot(sk_ref[hp], q_hp)
            scores.append(s)
            box = [jnp.full((k, tp), NEG_INF, F32)]

            def on_round(r, m, box=box):
                box[0] = jnp.where(row16 == r, m, box[0])

            ranks.append(_top_rounds(s, k, on_round, track_rank=(p == 1)))
            tops.append(box[0])
        t0, t1 = tops
        half = k // 2
        cand = jnp.concatenate([t0[0:1, :] + t1] + [t0[a:a + 1, :] + t1[0:half, :] for a in range(1, half)]
                               + [t0[half:, :] + t1[0:1, :]], axis=0)
        top = t0[0:1, :] + t1[0:1, :]
        acc = [jnp.zeros((1, tp), F32), top]

        def on_cand(r, m, acc=acc):
            acc[0] = acc[0] + jnp.exp(m - top)
            acc[1] = m

        _top_rounds(cand, k, on_cand, track_rank=False)
        z, thr = acc
        n = jnp.zeros((k, tp), F32)
        for b in range(k):
            n = n + jnp.where(t0 + t1[b:b + 1, :] >= thr, 1.0, 0.0)
        cnt = jnp.zeros((nk, tp), F32)
        for a in range(k):
            cnt = jnp.where(scores[0] == t0[a:a + 1, :], n[a:a + 1, :], cnt)
        rank1_ref[h] = ranks[1].astype(BF16)
        e1_ref[h] = jnp.exp(scores[1] - t1[0:1, :]).astype(BF16)
        cnt_ref[h] = cnt
        e0_ref[h] = jnp.exp(scores[0] - t0[0:1, :]) / z

    def head_pair(i, carry):
        one_head(2 * i)
        one_head(2 * i + 1)
        return carry

    lax.fori_loop(0, PEER_HEADS // 2, head_pair, 0)


def _peer_route(h, g, wq_t_bf16, sub_keys_bf16):
    t = h.shape[0]
    tp = ROUTE_TILE
    nk = PEER_N_KEYS
    nh = PEER_HEADS
    row = lambda i: (i, 0)
    lanes = lambda i: (0, 0, i)
    spec = pl.BlockSpec((nh, nk, tp), lanes)
    sd = lambda dt: jax.ShapeDtypeStruct((nh, nk, t), dt)
    return pl.pallas_call(
        _peer_route_kernel,
        grid=(t // tp,),
        in_specs=[pl.BlockSpec((tp, D_MODEL), row), _full((1, D_MODEL)), _full(wq_t_bf16.shape),
                  _full(sub_keys_bf16.shape)],
        out_specs=[pl.BlockSpec((tp, D_MODEL), row), spec, spec, spec, spec],
        out_shape=[jax.ShapeDtypeStruct((t, D_MODEL), BF16), sd(BF16), sd(BF16), sd(F32), sd(F32)],
        scratch_shapes=[pltpu.VMEM((nh * 2 * nk, tp), F32)],
        compiler_params=_cparams("parallel"),
        name="peer_route",
    )(h, g, wq_t_bf16, sub_keys_bf16)


E_TILE = 2048
DENSE_TILE = 512


BF16_ROWS = 16


def _rows_bf16(row, nrows):
    one = jnp.broadcast_to(row, (BF16_ROWS, row.shape[1])).astype(BF16)
    return jnp.concatenate([one] * (nrows // BF16_ROWS), axis=0)


def _peer_dense_kernel(xn_ref, u_ref, vt_ref, rank1_ref, e1_ref, cnt_ref, e0_ref, h_ref, out_ref, acc):
    e = pl.program_id(1)
    nk = PEER_N_KEYS
    tp = xn_ref.shape[0]

    @pl.when(e == 0)
    def _():
        acc[...] = jnp.zeros_like(acc)

    gates = []
    for ii in range(E_TILE // nk):
        i = e * (E_TILE // nk) + ii
        gate = jnp.zeros((nk, tp), BF16)
        for h in range(PEER_HEADS):
            cnt_row = _rows_bf16(cnt_ref[h, pl.ds(i, 1), :], nk)
            e0_row = _rows_bf16(e0_ref[h, pl.ds(i, 1), :], nk)
            gate = gate + jnp.where(rank1_ref[h] < cnt_row, e1_ref[h], jnp.zeros((), BF16)) * e0_row
        gates.append(gate)
    hid = _dot_nt(u_ref[...], xn_ref[...])
    hb = hid.astype(BF16)
    act = (0.5 * hb) * (1.0 + lax.erf(hb * (2.0 ** -0.5)))
    acc[...] += _dot(vt_ref[...], act * jnp.concatenate(gates, axis=0))

    @pl.when(e == pl.num_programs(1) - 1)
    def _():
        out_ref[...] = h_ref[...] + acc[...].T


def _peer_dense(h, xn, u_bf16, vt_bf16, rank1, e1, cnt, e0):
    t = h.shape[0]
    tp = min(DENSE_TILE, t)
    ne = u_bf16.shape[0]
    nh, nk = PEER_HEADS, PEER_N_KEYS
    tok = lambda i, e: (i, 0)
    lanes = lambda i, e: (0, 0, i)
    spec = pl.BlockSpec((nh, nk, tp), lanes)
    return pl.pallas_call(
        _peer_dense_kernel,
        grid=(t // tp, ne // E_TILE),
        in_specs=[pl.BlockSpec((tp, D_MODEL), tok), pl.BlockSpec((E_TILE, D_MODEL), lambda i, e: (e, 0)),
                  pl.BlockSpec((D_MODEL, E_TILE), lambda i, e: (0, e)), spec, spec, spec, spec,
                  pl.BlockSpec((tp, D_MODEL), tok)],
        out_specs=pl.BlockSpec((tp, D_MODEL), tok),
        out_shape=jax.ShapeDtypeStruct((t, D_MODEL), F32),
        scratch_shapes=[pltpu.VMEM((D_MODEL, tp), F32)],
        compiler_params=_cparams("parallel", "arbitrary"),
        name="peer_dense",
    )(xn, u_bf16, vt_bf16, rank1, e1, cnt, e0, h)


def _ple_kernel(h_ref, p_ref, g_ref, wg_ref, b_ref, wp_ref, fg_ref, out_ref, *, final):
    h = h_ref[...]
    gate = _sigmoid(_dot(_rms(h, g_ref[...]).astype(BF16), wg_ref[...]) + b_ref[...])
    out = h + gate * _dot(p_ref[...].astype(BF16), wp_ref[...])
    out_ref[...] = _rms(out, fg_ref[...]) if final else out


def _ple(h, p, g, wg_bf16, b, wp_bf16, final_g, final, tm=512):
    t = h.shape[0]
    row = lambda i: (i, 0)
    vec = _full((1, D_MODEL))
    return pl.pallas_call(
        functools.partial(_ple_kernel, final=final),
        grid=(t // tm,),
        in_specs=[pl.BlockSpec((tm, D_MODEL), row), pl.BlockSpec((tm, PLE_DIM), row), vec,
                  _full(wg_bf16.shape), vec, _full(wp_bf16.shape), vec],
        out_specs=pl.BlockSpec((tm, D_MODEL), row),
        out_shape=jax.ShapeDtypeStruct((t, D_MODEL), F32),
        compiler_params=_cparams("parallel"),
        name="ple_final" if final else "ple",
    )(h, p, g, wg_bf16, b, wp_bf16, final_g)


def _mixer(h, bsz, seq, lw, consts):
    qs, ks, vs, u = _inproj(h, lw["norm_mix_g"], lw["w_in"], consts["cos"], consts["sin"], seq)
    att_parts = [_attn_pattern(q, k, v, bsz, seq, dil) for q, k, v, dil in zip(qs, ks, vs, DILATIONS)]
    prep = _rwkv_prep(u, bsz, seq, lw["rwkv_shift"], lw["rwkv_w0"], lw["rwkv_w2"], lw["rwkv_a0"],
                      lw["rwkv_a2"], lw["rwkv_g2"], lw["rwkv_k_k"], lw["rwkv_k_a"], lw["rwkv_r_k"], consts)
    dir_ops, (rv, wdec, bonus, g) = prep[:N_DIR_OPERANDS], prep[N_DIR_OPERANDS:]
    yf, yb = _rwkv_scan(dir_ops, rv, wdec, bsz, seq, consts["scan_masks"])
    return _outproj(h, att_parts, lw["att_norm_g"], yf, yb, bonus, g, lw["rwkv_ln_w"], lw["rwkv_ln_b"],
                    consts["bd512"], lw["w_out"])


def _peer(h, lw):
    xn, rank1, e1, cnt, e0 = _peer_route(h, lw["norm_ffn_g"], lw["peer_w_q_t"], lw["peer_sub_keys"])
    return _peer_dense(h, xn, lw["peer_u"], lw["peer_v_t"], rank1, e1, cnt, e0)


def _layer_weights(i, wts):
    row = lambda a: a[i].reshape(1, -1)
    bf = lambda a: a[i].astype(BF16)
    return dict(
        norm_mix_g=row(wts["norm_mix_g"]), w_in=bf(wts["w_in"]), att_norm_g=row(wts["att_norm_g"]),
        rwkv_shift=wts["rwkv_shift"][i], rwkv_w0=wts["rwkv_w0"][i], rwkv_w2=bf(wts["rwkv_w2"]),
        rwkv_a0=wts["rwkv_a0"][i], rwkv_a2=bf(wts["rwkv_a2"]), rwkv_g2=bf(wts["rwkv_g2"]),
        rwkv_k_k=row(wts["rwkv_k_k"]), rwkv_k_a=row(wts["rwkv_k_a"]), rwkv_r_k=row(wts["rwkv_r_k"]),
        rwkv_ln_w=row(wts["rwkv_ln_w"]), rwkv_ln_b=row(wts["rwkv_ln_b"]), w_out=bf(wts["w_out"]),
        norm_ffn_g=row(wts["norm_ffn_g"]), peer_w_q_t=bf(wts["peer_w_q"]).T,
        peer_sub_keys=bf(wts["peer_sub_keys"]).reshape(2 * PEER_HEADS, PEER_N_KEYS, PEER_N_KEYS),
        peer_u=bf(wts["peer_u"]), peer_v_t=bf(wts["peer_v"]).T,
        ple_norm_g=row(wts["ple_norm_g"]), ple_w_gate=bf(wts["ple_w_gate"]), ple_b_gate=row(wts["ple_b_gate"]),
        ple_w_proj=bf(wts["ple_w_proj"]),
    )


def _trunk(x, p, wts, final_g):
    bsz, seq, _ = x.shape
    depth = p.shape[0]
    t = bsz * seq
    consts = _prep_consts()
    consts["cos"], consts["sin"] = _rope_tables(seq)
    consts["scan_masks"] = _scan_masks()
    h = x.reshape(t, D_MODEL)
    fg = final_g.reshape(1, D_MODEL)
    for i in range(depth):
        lw = _layer_weights(i, wts)
        h = _mixer(h, bsz, seq, lw, consts)
        h = _peer(h, lw)
        h = _ple(h, p[i].reshape(t, PLE_DIM), lw["ple_norm_g"], lw["ple_w_gate"], lw["ple_b_gate"],
                 lw["ple_w_proj"], fg, final=(i == depth - 1))
    return h.reshape(bsz, seq, D_MODEL)


def kernel(x_prompt, x_sample, p_prompt, p_sample, norm_mix_g, w_in, att_norm_g, rwkv_shift, rwkv_w0, rwkv_w2, rwkv_a0, rwkv_a2, rwkv_g2, rwkv_k_k, rwkv_k_a, rwkv_r_k, rwkv_ln_w, rwkv_ln_b, w_out, norm_ffn_g, peer_w_q, peer_sub_keys, peer_u, peer_v, ple_norm_g, ple_w_gate, ple_b_gate, ple_w_proj, final_norm_g):
    wts = dict(norm_mix_g=norm_mix_g, w_in=w_in, att_norm_g=att_norm_g, rwkv_shift=rwkv_shift, rwkv_w0=rwkv_w0,
               rwkv_w2=rwkv_w2, rwkv_a0=rwkv_a0, rwkv_a2=rwkv_a2, rwkv_g2=rwkv_g2, rwkv_k_k=rwkv_k_k,
               rwkv_k_a=rwkv_k_a, rwkv_r_k=rwkv_r_k, rwkv_ln_w=rwkv_ln_w, rwkv_ln_b=rwkv_ln_b, w_out=w_out,
               norm_ffn_g=norm_ffn_g, peer_w_q=peer_w_q, peer_sub_keys=peer_sub_keys, peer_u=peer_u,
               peer_v=peer_v, ple_norm_g=ple_norm_g, ple_w_gate=ple_w_gate, ple_b_gate=ple_b_gate,
               ple_w_proj=ple_w_proj)
    nb = x_prompt.shape[0]
    x = jnp.concatenate([x_prompt, x_sample], axis=0)
    p = jnp.concatenate([p_prompt, p_sample], axis=1)
    y = _trunk(x, p, wts, final_norm_g)
    return (y[:nb], y[nb:])
```

```python
import functools

import jax
import jax.numpy as jnp
import numpy as np
from jax import lax
from jax.experimental import pallas as pl
from jax.experimental.pallas import tpu as pltpu

F32 = jnp.float32
BF16 = jnp.bfloat16

D_MODEL = 1024
HEAD_DIM = 64
ATT_WIDTH = 512
ATT_HEADS = 8
RWKV_WIDTH = 512
RWKV_COLS = 1920
QKV_COLS = 3 * ATT_WIDTH
DILATIONS = (1, 4, 16)
HALF_WINDOW = 64
ROPE_THETA = 500000.0
ROPE_DIM = 16
NORM_EPS = 1e-6
LN_X_EPS = 64e-5
NEG_INF = -1e30
PEER_HEADS = 8
PEER_N_KEYS = 128
PEER_TOPK = 16
PLE_DIM = 256
CHUNK = 64
GROUP = 256

VMEM_LIMIT = 56 * 1024 * 1024


def _cparams(*sem, flags=None):
    return pltpu.CompilerParams(dimension_semantics=sem, vmem_limit_bytes=VMEM_LIMIT, flags=flags)


def _full(shape):
    nd = len(shape)
    return pl.BlockSpec(shape, lambda *_: (0,) * nd)


def _dot(a, b):
    return jnp.dot(a, b, preferred_element_type=F32)


def _dot_nt(a, b):
    return lax.dot_general(a, b, (((1,), (1,)), ((), ())), preferred_element_type=F32)


def _dot_tn(a, b):
    return lax.dot_general(a, b, (((0,), (0,)), ((), ())), preferred_element_type=F32)


def _split3(x):
    hi = x.astype(BF16)
    r1 = x - hi.astype(F32)
    mid = r1.astype(BF16)
    lo = (r1 - mid.astype(F32)).astype(BF16)
    return hi, mid, lo


def _dot_rexact(x, r_bf16):
    hi, mid, lo = _split3(x)
    return _dot(hi, r_bf16) + _dot(mid, r_bf16) + _dot(lo, r_bf16)


def _rms(x, g):
    return x * lax.rsqrt(jnp.mean(x * x, axis=-1, keepdims=True) + NORM_EPS) * g


def _sigmoid(x):
    return 1.0 / (1.0 + jnp.exp(-x))


LANES = 128
ATT_LANE_TILES = ATT_WIDTH // LANES


def _to_classes(t, stage, out_refs):
    rows = t.shape[0]
    for c in range(ATT_LANE_TILES):
        stage[c] = t[:, c * LANES:(c + 1) * LANES]
    for dil, ref in zip(DILATIONS, out_refs):
        if dil == 1:
            ref[...] = t.astype(ref.dtype)
            continue
        for r in range(dil):
            for c in range(ATT_LANE_TILES):
                lo = r * ATT_WIDTH + c * LANES
                ref[:, lo:lo + LANES] = stage[c, pl.ds(r, rows // dil, stride=dil), :].astype(ref.dtype)


def _from_classes(ref, dil, stage):
    if dil == 1:
        return ref[...]
    rows = ref.shape[0] * dil
    for r in range(dil):
        for c in range(ATT_LANE_TILES):
            lo = r * ATT_WIDTH + c * LANES
            stage[c, pl.ds(r, rows // dil, stride=dil), :] = ref[:, lo:lo + LANES]
    return jnp.concatenate([stage[c] for c in range(ATT_LANE_TILES)], axis=1)


def _inproj_kernel(h_ref, g_ref, w_ref, cos_ref, sin_ref, *refs):
    nd = len(DILATIONS)
    q_refs, k_refs, v_refs = refs[:nd], refs[nd:2 * nd], refs[2 * nd:3 * nd]
    u_ref, stage = refs[3 * nd:]
    xn = _rms(h_ref[...], g_ref[...]).astype(BF16)
    qkv = _dot(xn, w_ref[:, :QKV_COLS])
    u_ref[...] = _dot(xn, w_ref[:, QKV_COLS:])
    cos = cos_ref[...]
    sin = sin_ref[...]
    lane = lax.broadcasted_iota(jnp.int32, cos.shape, 1) % HEAD_DIM
    half = ROPE_DIM // 2

    def rot(t):
        partner = jnp.where(lane < half, pltpu.roll(t, ATT_WIDTH - half, 1), pltpu.roll(t, half, 1))
        return t * cos + partner * sin

    _to_classes(rot(qkv[:, :ATT_WIDTH]) * (HEAD_DIM ** -0.5), stage, q_refs)
    _to_classes(rot(qkv[:, ATT_WIDTH:2 * ATT_WIDTH]), stage, k_refs)
    _to_classes(qkv[:, 2 * ATT_WIDTH:], stage, v_refs)


def _inproj(h, g, w_bf16, cos_t, sin_t, seq, tm=512):
    t = h.shape[0]
    tm = min(tm, seq)
    nseq = seq // tm
    row = lambda i: (i, 0)
    pos = lambda i: (i % nseq, 0)
    cls_specs = [pl.BlockSpec((tm // d, d * ATT_WIDTH), row) for d in DILATIONS]
    cls_shapes = [jax.ShapeDtypeStruct((t // d, d * ATT_WIDTH), BF16) for d in DILATIONS]
    outs = pl.pallas_call(
        _inproj_kernel,
        grid=(t // tm,),
        in_specs=[pl.BlockSpec((tm, D_MODEL), row), _full((1, D_MODEL)), _full(w_bf16.shape),
                  pl.BlockSpec((tm, ATT_WIDTH), pos), pl.BlockSpec((tm, ATT_WIDTH), pos)],
        out_specs=cls_specs * 3 + [pl.BlockSpec((tm, RWKV_COLS), row)],
        out_shape=cls_shapes * 3 + [jax.ShapeDtypeStruct((t, RWKV_COLS), F32)],
        scratch_shapes=[pltpu.VMEM((ATT_LANE_TILES, tm, LANES), F32)],
        compiler_params=_cparams("parallel"),
        name="inproj",
    )(h, g, w_bf16, cos_t, sin_t)
    nd = len(DILATIONS)
    return outs[:nd], outs[nd:2 * nd], outs[2 * nd:3 * nd], outs[3 * nd]


def _rope_tables(seq):
    half = ROPE_DIM // 2
    inv_freq = ROPE_THETA ** (-jnp.arange(half, dtype=F32) / half)
    ang = jnp.arange(seq, dtype=F32)[:, None] * inv_freq[None, :]
    ones = jnp.ones((seq, HEAD_DIM - ROPE_DIM), F32)
    cos_h = jnp.concatenate([jnp.cos(ang), jnp.cos(ang), ones], axis=1)
    sin_h = jnp.concatenate([-jnp.sin(ang), jnp.sin(ang), 0.0 * ones], axis=1)
    return jnp.tile(cos_h, (1, ATT_HEADS)), jnp.tile(sin_h, (1, ATT_HEADS))


ATT_SUB = 128


def _attn_kernel(q_ref, kp_ref, kc_ref, kn_ref, vp_ref, vc_ref, vn_ref, o_ref, lse_ref,
                 kbuf, vbuf, *, length, tq):
    j = pl.program_id(2)
    hw = HALF_WINDOW
    kbuf[0:hw] = kp_ref[0]
    kbuf[hw:hw + tq] = kc_ref[0]
    kbuf[hw + tq:] = kn_ref[0]
    vbuf[0:hw] = vp_ref[0]
    vbuf[hw:hw + tq] = vc_ref[0]
    vbuf[hw + tq:] = vn_ref[0]
    nk = ATT_SUB + 2 * hw
    qi = lax.broadcasted_iota(jnp.int32, (ATT_SUB, nk), 0)
    ki = lax.broadcasted_iota(jnp.int32, (ATT_SUB, nk), 1)
    band = jnp.abs(ki - hw - qi) <= hw
    heads = [slice(h * HEAD_DIM, (h + 1) * HEAD_DIM) for h in range(ATT_HEADS)]
    blocks = [(s * ATT_SUB, c) for s in range(tq // ATT_SUB) for c in heads]
    sc = jnp.concatenate([_dot_nt(q_ref[0, q0:q0 + ATT_SUB, c], kbuf[q0:q0 + nk, c]) for q0, c in blocks], axis=0)
    valid = []
    for s in range(tq // ATT_SUB):
        kpos = j * tq + (s * ATT_SUB - hw) + ki
        valid += [band & (kpos >= 0) & (kpos < length)] * ATT_HEADS
    sc = jnp.where(jnp.concatenate(valid, axis=0), sc, NEG_INF)
    m = jnp.max(sc, axis=-1, keepdims=True)
    p = jnp.exp(sc - m)
    l = jnp.sum(p, axis=-1, keepdims=True)
    pb = p.astype(BF16)
    inv_l = 1.0 / l
    lse = m + jnp.log(l)
    for n, (q0, c) in enumerate(blocks):
        rows = slice(n * ATT_SUB, (n + 1) * ATT_SUB)
        o_ref[0, q0:q0 + ATT_SUB, c] = _dot(pb[rows], vbuf[q0:q0 + nk, c]) * inv_l[rows]
        lse_ref[0, q0:q0 + ATT_SUB, c] = jnp.broadcast_to(lse[rows], (ATT_SUB, HEAD_DIM))


def _attn_pattern(q, k, v, bsz, seq, dil):
    length = seq // dil
    tq = min(length, 512)
    nq = length // tq
    nh = length // HALF_WINDOW
    rq = tq // HALF_WINDOW
    view = lambda a: a.reshape(bsz, length, dil * ATT_WIDTH)
    cur = pl.BlockSpec((1, tq, ATT_WIDTH), lambda b, r, j: (b, j, r))
    prev = pl.BlockSpec((1, HALF_WINDOW, ATT_WIDTH), lambda b, r, j: (b, jnp.maximum(j * rq - 1, 0), r))
    nxt = pl.BlockSpec((1, HALF_WINDOW, ATT_WIDTH), lambda b, r, j: (b, jnp.minimum((j + 1) * rq, nh - 1), r))
    out_sd = jax.ShapeDtypeStruct((bsz, length, dil * ATT_WIDTH), F32)
    o, lse = pl.pallas_call(
        functools.partial(_attn_kernel, length=length, tq=tq),
        grid=(bsz, dil, nq),
        in_specs=[cur, prev, cur, nxt, prev, cur, nxt],
        out_specs=[cur, cur],
        out_shape=[out_sd, out_sd],
        scratch_shapes=[pltpu.VMEM((tq + 2 * HALF_WINDOW, ATT_WIDTH), BF16)] * 2,
        compiler_params=_cparams("parallel", "parallel", "parallel"),
        name=f"attn_d{dil}",
    )(view(q), view(k), view(k), view(k), view(v), view(v), view(v))
    flat = (bsz * length, dil * ATT_WIDTH)
    return o.reshape(flat), lse.reshape(flat)


R_TILE = 256
R_COLS = dict(r=0, k=512, v=1024, wd=1536, ad=1664, gd=1792)
DECAY_LOG_SCALE = float(np.exp(-0.5))


def _rwkv_prep_kernel(uc_ref, up_ref, un_ref, shift_ref, w0_ref, w2_ref, a0_ref, a2_ref, g2_ref,
                      kk_ref, ka_ref, rk_ref, bd_ref, tri_ref, sel_ref,
                      rt_ref, at_ref, bt_ref, kt_ref, bh_ref, kh_ref, v_ref, wdec_ref, bonus_ref, g_ref):
    j = pl.program_id(1)
    nj = pl.num_programs(1)
    tr = R_TILE
    cur = uc_ref[0]
    row = lax.broadcasted_iota(jnp.int32, cur.shape, 0)
    prev_edge = up_ref[0, 7:8, :] * (j > 0).astype(F32)
    next_edge = un_ref[0, 0:1, :] * (j < nj - 1).astype(F32)
    prev = jnp.where(row == 0, prev_edge, pltpu.roll(cur, 1, 0))
    nxt = jnp.where(row == tr - 1, next_edge, pltpu.roll(cur, tr - 1, 0))
    us = shift_ref[0:1, :] * prev + shift_ref[1:2, :] * cur + shift_ref[2:3, :] * nxt

    w = RWKV_WIDTH
    r = us[:, R_COLS["r"]:R_COLS["r"] + w]
    k = us[:, R_COLS["k"]:R_COLS["k"] + w]
    v = us[:, R_COLS["v"]:R_COLS["v"] + w]
    gd = us[:, R_COLS["gd"]:]
    bd = bd_ref[...]

    g_ref[...] = _dot(_sigmoid(gd).astype(BF16), g2_ref[...])
    v_ref[...] = v.astype(BF16)
    kk = k * kk_ref[...]
    kk = kk / jnp.maximum(jnp.sqrt(_dot_rexact(kk * kk, bd)), 1e-12)

    nchunk = tr // CHUNK
    ksum = jnp.zeros_like(k)
    for d in range(2):
        wd = us[:, R_COLS["wd"] + 64 * d:R_COLS["wd"] + 64 * (d + 1)]
        ad = us[:, R_COLS["ad"] + 64 * d:R_COLS["ad"] + 64 * (d + 1)]
        z = w0_ref[d:d + 1, :] + _dot(jnp.tanh(wd).astype(BF16), w2_ref[d])
        lw = -DECAY_LOG_SCALE * _sigmoid(z)
        a = _sigmoid(a0_ref[d:d + 1, :] + _dot(ad.astype(BF16), a2_ref[d]))
        kd = k * (1.0 + (a - 1.0) * ka_ref[...])
        ksum = ksum + kd
        b = kk * a
        lw3 = _split3(lw)
        sum3 = lambda l: _dot(l, lw3[0]) + _dot(l, lw3[1]) + _dot(l, lw3[2])
        c = sum3(tri_ref[d])
        tot = sum3(sel_ref[...])
        cend = jnp.concatenate(
            [jnp.broadcast_to(tot[i:i + 1, :], (CHUNK, w)) for i in range(nchunk)], axis=0)
        e_c = jnp.exp(c)
        e_nc = jnp.exp(-c)
        e_rem = jnp.exp(cend - c)
        rt_ref[d] = (r * e_c).astype(BF16)
        at_ref[d] = (-kk * jnp.exp(c - lw)).astype(BF16)
        bt_ref[d] = (b * e_nc).astype(BF16)
        kt_ref[d] = (kd * e_nc).astype(BF16)
        bh_ref[d] = (b * e_rem).astype(BF16)
        kh_ref[d] = (kd * e_rem).astype(BF16)
        wdec = jnp.exp(tot)
        for i in range(nchunk):
            wdec_ref[d, 0, i] = wdec[i:i + 1, :]
    bonus_ref[...] = _dot_rexact(r * ksum * rk_ref[...], bd) * v


def _prep_consts():
    tr = R_TILE
    idx = np.arange(tr)
    same = (idx[:, None] // CHUNK) == (idx[None, :] // CHUNK)
    tri = np.stack([same & (idx[:, None] >= idx[None, :]), same & (idx[:, None] <= idx[None, :])])
    sel = (np.arange(8)[:, None] == (idx[None, :] // CHUNK))
    hd = np.arange(RWKV_WIDTH) // HEAD_DIM
    bd = hd[:, None] == hd[None, :]
    as_bf16 = lambda m: jnp.asarray(m.astype(np.float32), dtype=BF16)
    return dict(tri=as_bf16(tri), sel=as_bf16(sel), bd512=as_bf16(bd))


def _rwkv_prep(u, bsz, seq, shift_w, w0, w2, a0, a2, g2, k_k, k_a, r_k, consts):
    tr = R_TILE
    nj = seq // tr
    t = bsz * seq
    w = RWKV_WIDTH
    u3 = u.reshape(bsz, seq, RWKV_COLS)
    rows8 = tr // 8
    cur = pl.BlockSpec((1, tr, RWKV_COLS), lambda b, j: (b, j, 0))
    prev = pl.BlockSpec((1, 8, RWKV_COLS), lambda b, j: (b, jnp.maximum(j * rows8 - 1, 0), 0))
    nxt = pl.BlockSpec((1, 8, RWKV_COLS), lambda b, j: (b, jnp.minimum((j + 1) * rows8, seq // 8 - 1), 0))
    tok2 = pl.BlockSpec((2, tr, w), lambda b, j: (0, b * nj + j, 0))
    tok = pl.BlockSpec((tr, w), lambda b, j: (b * nj + j, 0))
    nchunk = tr // CHUNK
    wdec_spec = pl.BlockSpec((2, 1, nchunk, 1, w), lambda b, j: (0, b, j, 0, 0))
    small = [shift_w, w0, w2, a0, a2, g2, k_k, k_a, r_k, consts["bd512"], consts["tri"], consts["sel"]]
    sd2 = jax.ShapeDtypeStruct((2, t, w), BF16)
    return pl.pallas_call(
        _rwkv_prep_kernel,
        grid=(bsz, nj),
        in_specs=[cur, prev, nxt] + [_full(a.shape) for a in small],
        out_specs=[tok2] * 6 + [tok, wdec_spec, tok, tok],
        out_shape=[sd2] * 6 + [jax.ShapeDtypeStruct((t, w), BF16),
                               jax.ShapeDtypeStruct((2, bsz, seq // CHUNK, 1, w), F32),
                               jax.ShapeDtypeStruct((t, w), F32), jax.ShapeDtypeStruct((t, w), F32)],
        compiler_params=_cparams("parallel", "parallel"),
        name="rwkv_prep",
    )(u3, u3, u3, *small)


S_TILE = 128
LEVELS = (2, 4, 8, 16, 32)
N_DIR_OPERANDS = 6


def _rwkv_scan_kernel(*refs):
    nop = N_DIR_OPERANDS
    dir_refs = refs[:2 * nop]
    vf_ref, vb_ref, wf_ref, wb_ref, mask_ref, yf_ref, yb_ref, state = refs[2 * nop:]
    j = pl.program_id(1)

    @pl.when(j == 0)
    def _():
        state[...] = jnp.zeros_like(state)

    nchunk = S_TILE // CHUNK
    nh = GROUP // HEAD_DIM
    ng = RWKV_WIDTH // GROUP
    bdm = mask_ref[0]
    bdm_f32 = bdm.astype(F32)
    eye = mask_ref[1].astype(F32)
    tile4 = lambda x: jnp.concatenate([x] * nh, axis=0)
    io = ((vf_ref, wf_ref, yf_ref), (vb_ref, wb_ref, yb_ref))
    order = [(d, g, ci if d == 0 else nchunk - 1 - ci)
             for ci in range(nchunk) for g in range(ng) for d in range(2)]

    pre = {}
    for d, g, c in order:
        rt_ref, at_ref, bt_ref, kt_ref, bh_ref, kh_ref = [dir_refs[2 * i + d] for i in range(nop)]
        rows = (0, slice(c * CHUNK, (c + 1) * CHUNK), slice(g * GROUP, (g + 1) * GROUP))
        a_bd = tile4(at_ref[rows]) * bdm
        r_bd = tile4(rt_ref[rows]) * bdm
        v_bd = tile4(io[d][0][rows]) * bdm
        ar = jnp.concatenate([a_bd, r_bd], axis=0)
        ar_b = _dot_nt(ar, tile4(bt_ref[rows]))
        ar_k = _dot_nt(ar, tile4(kt_ref[rows]))
        strict = mask_ref[2 + d]
        incl = mask_ref[4 + d]
        a_ab = ar_b[:GROUP].astype(BF16) * strict
        a_ak = ar_k[:GROUP].astype(BF16) * strict
        a_r = jnp.concatenate([ar_b[GROUP:].astype(BF16) * incl, ar_k[GROUP:].astype(BF16) * incl], axis=1)
        hat = jnp.concatenate([tile4(bh_ref[rows]), tile4(kh_ref[rows])], axis=0)
        pre[d, g, c] = dict(a_bd=a_bd, r_bd=r_bd, v_bd=v_bd, a_ab=a_ab, a_ak=a_ak, a_r=a_r, hat=hat,
                            tinv=eye + (a_ab * mask_ref[6 + d]).astype(F32))
    for li in range(len(LEVELS)):
        for key in order:
            p = pre[key]
            lm = p["a_ab"] * mask_ref[8 + 2 * li + key[0]]
            tb = p["tinv"].astype(BF16)
            p["tinv"] = p["tinv"] + _dot(tb, _dot(lm, tb).astype(BF16))
    s = {(d, g): state[d, g] for d in range(2) for g in range(ng)}
    for d, g, c in order:
        p = pre[d, g, c]
        w_ref, y_ref = io[d][1], io[d][2]
        cols = slice(g * GROUP, (g + 1) * GROUP)
        s_bf = s[d, g].astype(BF16)
        rhs = _dot_nt(p["a_bd"], s_bf) + _dot(p["a_ak"], p["v_bd"])
        u_bf = _dot(p["tinv"].astype(BF16), rhs.astype(BF16)).astype(BF16)
        uv = jnp.concatenate([u_bf, p["v_bd"]], axis=0)
        y = _dot_nt(p["r_bd"], s_bf) + _dot(p["a_r"], uv)
        y_nat = y[0:CHUNK]
        for hh in range(1, nh):
            y_nat = y_nat + y[hh * CHUNK:(hh + 1) * CHUNK]
        y_ref[0, c * CHUNK:(c + 1) * CHUNK, cols] = y_nat
        s[d, g] = s[d, g] * w_ref[0, 0, c, :, cols] + _dot_tn(uv, p["hat"]) * bdm_f32
    for d in range(2):
        for g in range(ng):
            state[d, g] = s[d, g]


def _scan_masks():
    n = GROUP
    idx = np.arange(n)
    head = idx // CHUNK
    t = idx % CHUNK
    same = head[:, None] == head[None, :]
    tt, ss = t[:, None], t[None, :]
    masks = [same, np.eye(n, dtype=bool)]
    masks += [same & (tt > ss), same & (tt < ss)]
    masks += [same & (tt >= ss), same & (tt <= ss)]
    for m in (1,) + LEVELS:
        blk = same & (tt // (2 * m) == ss // (2 * m))
        lower = blk & ((tt // m) % 2 == 1) & ((ss // m) % 2 == 0)
        upper = blk & ((tt // m) % 2 == 0) & ((ss // m) % 2 == 1)
        masks += [lower, upper]
    return jnp.asarray(np.stack(masks).astype(np.float32), dtype=BF16)


def _rwkv_scan(dir_ops, v, wdec, bsz, seq, masks):
    st = S_TILE
    nj = seq // st
    ng = RWKV_WIDTH // GROUP
    w = RWKV_WIDTH
    t = bsz * seq
    nchunk = st // CHUNK
    blk = (1, st, w)
    fwd = lambda d: pl.BlockSpec(blk, lambda b, j: (d, b * nj + j, 0))
    bwd = lambda d: pl.BlockSpec(blk, lambda b, j: (d, b * nj + (nj - 1 - j), 0))
    wf = pl.BlockSpec((1, 1, nchunk, 1, w), lambda b, j: (0, b, j, 0, 0))
    wb = pl.BlockSpec((1, 1, nchunk, 1, w), lambda b, j: (1, b, nj - 1 - j, 0, 0))
    v3 = v.reshape(1, t, w)
    in_specs, args = [], []
    for arr in dir_ops:
        in_specs += [fwd(0), bwd(1)]
        args += [arr, arr]
    in_specs += [fwd(0), bwd(0), wf, wb, _full(masks.shape)]
    args += [v3, v3, wdec, wdec, masks]
    sd = jax.ShapeDtypeStruct((1, t, w), F32)
    yf, yb = pl.pallas_call(
        _rwkv_scan_kernel,
        grid=(bsz, nj),
        in_specs=in_specs,
        out_specs=[fwd(0), bwd(0)],
        out_shape=[sd, sd],
        scratch_shapes=[pltpu.VMEM((2, ng, GROUP, GROUP), F32)],
        compiler_params=_cparams("parallel", "arbitrary"),
        name="rwkv_scan",
    )(*args)
    return yf.reshape(t, w), yb.reshape(t, w)


def _outproj_kernel(h_ref, o1_ref, o2_ref, o3_ref, l1_ref, l2_ref, l3_ref, attg_ref,
                    yf_ref, yb_ref, bonus_ref, g_ref, lnw_ref, lnb_ref, bd_ref, w_ref, out_ref, stage):
    bd = bd_ref[...]
    inv_n = 1.0 / HEAD_DIM
    nd = len(DILATIONS)
    o1, o2, o3 = [_from_classes(r, d, stage.at[i])
                  for i, (r, d) in enumerate(zip((o1_ref, o2_ref, o3_ref), DILATIONS))]
    l1, l2, l3 = [_from_classes(r, d, stage.at[nd + i])
                  for i, (r, d) in enumerate(zip((l1_ref, l2_ref, l3_ref), DILATIONS))]
    m = jnp.maximum(jnp.maximum(l1, l2), l3)
    w1, w2, w3 = jnp.exp(l1 - m), jnp.exp(l2 - m), jnp.exp(l3 - m)
    o = (w1 * o1 + w2 * o2 + w3 * o3) / (w1 + w2 + w3)
    att = o * lax.rsqrt(_dot_rexact(o * o, bd) * inv_n + NORM_EPS) * attg_ref[...]
    y = yf_ref[...] + yb_ref[...]
    yc = y - _dot_rexact(y, bd) * inv_n
    var = _dot_rexact(yc * yc, bd) * inv_n
    yn = yc * lax.rsqrt(var + LN_X_EPS) * lnw_ref[...] + lnb_ref[...]
    rw = (yn + bonus_ref[...]) * g_ref[...]
    out_ref[...] = (h_ref[...] + _dot(att.astype(BF16), w_ref[:ATT_WIDTH, :])
                    + _dot(rw.astype(BF16), w_ref[ATT_WIDTH:, :]))


def _outproj(h, att_parts, att_g, yf, yb, bonus, g, ln_w, ln_b, bd512, w_out_bf16, tm=256):
    t = h.shape[0]
    row = lambda i: (i, 0)
    half = pl.BlockSpec((tm, ATT_WIDTH), row)
    cls = [pl.BlockSpec((tm // d, d * ATT_WIDTH), row) for d in DILATIONS]
    (o1, l1), (o2, l2), (o3, l3) = att_parts
    vec = _full((1, ATT_WIDTH))
    return pl.pallas_call(
        _outproj_kernel,
        grid=(t // tm,),
        in_specs=[pl.BlockSpec((tm, D_MODEL), row)] + cls * 2 + [vec] + [half] * 4 + [vec, vec,
                  _full(bd512.shape), _full(w_out_bf16.shape)],
        out_specs=pl.BlockSpec((tm, D_MODEL), row),
        out_shape=jax.ShapeDtypeStruct((t, D_MODEL), F32),
        scratch_shapes=[pltpu.VMEM((2 * len(DILATIONS), ATT_LANE_TILES, tm, LANES), F32)],
        compiler_params=_cparams("parallel"),
        name="outproj",
    )(h, o1, o2, o3, l1, l2, l3, att_g, yf, yb, bonus, g, ln_w, ln_b, bd512, w_out_bf16)


ROUTE_TILE = 256


def _oddeven_merge_sort(n):
    pairs = []
    p = 1
    while p < n:
        k = p
        while k >= 1:
            for j in range(k % p, n - k, 2 * k):
                for i in range(min(k, n - j - k)):
                    if (i + j) // (2 * p) == (i + j + k) // (2 * p):
                        pairs.append((i + j, i + j + k))
            k //= 2
        p *= 2
    return tuple(pairs)


SUBLANES = 8


def _top_values(s, k, on_round):
    nv = s.shape[0] // SUBLANES
    x = [s[SUBLANES * v:SUBLANES * (v + 1), :] for v in range(nv)]
    for a, b in _oddeven_merge_sort(nv):
        x[a], x[b] = jnp.maximum(x[a], x[b]), jnp.minimum(x[a], x[b])
    for r in range(k):
        m = jnp.max(x[0], axis=0, keepdims=True)
        on_round(r, m)
        if r + 1 < k:
            hit = x[0] == m
            depth = min(nv - 1, k - 1 - r)
            for v in range(depth):
                x[v] = jnp.where(hit, x[v + 1], x[v])
            x[depth] = jnp.where(hit, NEG_INF, x[depth])


def _top_rounds(cur, nrounds, on_round, track_rank):
    rank = jnp.full(cur.shape, float(nrounds), F32) if track_rank else None
    for r in range(nrounds):
        m = jnp.max(cur, axis=0, keepdims=True)
        on_round(r, m)
        hit = cur == m
        if track_rank:
            rank = jnp.where(hit, float(r), rank)
        if r + 1 < nrounds:
            cur = jnp.where(hit, NEG_INF, cur)
    return rank


def _peer_route_kernel(h_ref, g_ref, wq_ref, sk_ref, xn_ref, rank1_ref, e1_ref, cnt_ref, e0_ref, qt):
    tp = ROUTE_TILE
    nk = PEER_N_KEYS
    k = PEER_TOPK
    xn = _rms(h_ref[...], g_ref[...]).astype(BF16)
    xn_ref[...] = xn
    qt[...] = _dot_nt(wq_ref[...], xn)
    row16 = lax.broadcasted_iota(jnp.int32, (k, tp), 0)

    def one_head(h):
        tops, scores, ranks = [], [], []
        for p in range(2):
            hp = 2 * h + p
            q_hp = qt[pl.ds(pl.multiple_of(hp * nk, nk), nk), :].astype(BF16)
            s = _dot(sk_ref[hp], q_hp)
            scores.append(s)
            box = [jnp.full((k, tp), NEG_INF, F32)]

            def on_round(r, m, box=box):
                box[0] = jnp.where(row16 == r, m, box[0])

            if p == 0:
                _top_values(s, k, on_round)
                ranks.append(None)
            else:
                ranks.append(_top_rounds(s, k, on_round, track_rank=True))
            tops.append(box[0])
        t0, t1 = tops
        half = k // 2
        cand = jnp.concatenate([t0[0:1, :] + t1] + [t0[a:a + 1, :] + t1[0:half, :] for a in range(1, half)]
                               + [t0[half:, :] + t1[0:1, :]], axis=0)
        top = t0[0:1, :] + t1[0:1, :]
        acc = [jnp.zeros((1, tp), F32), top]

        def on_cand(r, m, acc=acc):
            acc[0] = acc[0] + jnp.exp(m - top)
            acc[1] = m

        _top_rounds(cand, k, on_cand, track_rank=False)
        z, thr = acc
        n = jnp.zeros((k, tp), F32)
        for b in range(k):
            n = n + jnp.where(t0 + t1[b:b + 1, :] >= thr, 1.0, 0.0)
        cnt = jnp.zeros((nk, tp), F32)
        for a in range(k):
            cnt = jnp.where(scores[0] == t0[a:a + 1, :], n[a:a + 1, :], cnt)
        rank1_ref[h] = ranks[1].astype(BF16)
        e1_ref[h] = jnp.exp(scores[1] - t1[0:1, :]).astype(BF16)
        cnt_ref[h] = cnt
        e0_ref[h] = jnp.exp(scores[0] - t0[0:1, :]) / z

    def head_pair(i, carry):
        one_head(2 * i)
        one_head(2 * i + 1)
        return carry

    lax.fori_loop(0, PEER_HEADS // 2, head_pair, 0)


def _peer_route(h, g, wq_t_bf16, sub_keys_bf16):
    t = h.shape[0]
    tp = ROUTE_TILE
    nk = PEER_N_KEYS
    nh = PEER_HEADS
    row = lambda i: (i, 0)
    lanes = lambda i: (0, 0, i)
    spec = pl.BlockSpec((nh, nk, tp), lanes)
    sd = lambda dt: jax.ShapeDtypeStruct((nh, nk, t), dt)
    return pl.pallas_call(
        _peer_route_kernel,
        grid=(t // tp,),
        in_specs=[pl.BlockSpec((tp, D_MODEL), row), _full((1, D_MODEL)), _full(wq_t_bf16.shape),
                  _full(sub_keys_bf16.shape)],
        out_specs=[pl.BlockSpec((tp, D_MODEL), row), spec, spec, spec, spec],
        out_shape=[jax.ShapeDtypeStruct((t, D_MODEL), BF16), sd(BF16), sd(BF16), sd(F32), sd(F32)],
        scratch_shapes=[pltpu.VMEM((nh * 2 * nk, tp), F32)],
        compiler_params=_cparams("parallel"),
        name="peer_route",
    )(h, g, wq_t_bf16, sub_keys_bf16)


E_TILE = 2048
DENSE_TILE = 512


BF16_ROWS = 16


def _rows_bf16(row, nrows):
    one = jnp.broadcast_to(row, (BF16_ROWS, row.shape[1])).astype(BF16)
    return jnp.concatenate([one] * (nrows // BF16_ROWS), axis=0)


def _peer_dense_kernel(xn_ref, u_ref, vt_ref, rank1_ref, e1_ref, cnt_ref, e0_ref, h_ref, out_ref, acc):
    e = pl.program_id(1)
    nk = PEER_N_KEYS
    tp = xn_ref.shape[0]

    @pl.when(e == 0)
    def _():
        acc[...] = jnp.zeros_like(acc)

    gates = []
    for ii in range(E_TILE // nk):
        i = e * (E_TILE // nk) + ii
        gate = jnp.zeros((nk, tp), BF16)
        for h in range(PEER_HEADS):
            cnt_row = _rows_bf16(cnt_ref[h, pl.ds(i, 1), :], nk)
            e0_row = _rows_bf16(e0_ref[h, pl.ds(i, 1), :], nk)
            gate = gate + jnp.where(rank1_ref[h] < cnt_row, e1_ref[h], jnp.zeros((), BF16)) * e0_row
        gates.append(gate)
    hid = _dot_nt(u_ref[...], xn_ref[...])
    hb = hid.astype(BF16)
    act = (0.5 * hb) * (1.0 + lax.erf(hb * (2.0 ** -0.5)))
    acc[...] += _dot(vt_ref[...], act * jnp.concatenate(gates, axis=0))

    @pl.when(e == pl.num_programs(1) - 1)
    def _():
        out_ref[...] = h_ref[...] + acc[...].T


def _peer_dense(h, xn, u_bf16, vt_bf16, rank1, e1, cnt, e0):
    t = h.shape[0]
    tp = min(DENSE_TILE, t)
    ne = u_bf16.shape[0]
    nh, nk = PEER_HEADS, PEER_N_KEYS
    tok = lambda i, e: (i, 0)
    lanes = lambda i, e: (0, 0, i)
    spec = pl.BlockSpec((nh, nk, tp), lanes)
    return pl.pallas_call(
        _peer_dense_kernel,
        grid=(t // tp, ne // E_TILE),
        in_specs=[pl.BlockSpec((tp, D_MODEL), tok), pl.BlockSpec((E_TILE, D_MODEL), lambda i, e: (e, 0)),
                  pl.BlockSpec((D_MODEL, E_TILE), lambda i, e: (0, e)), spec, spec, spec, spec,
                  pl.BlockSpec((tp, D_MODEL), tok)],
        out_specs=pl.BlockSpec((tp, D_MODEL), tok),
        out_shape=jax.ShapeDtypeStruct((t, D_MODEL), F32),
        scratch_shapes=[pltpu.VMEM((D_MODEL, tp), F32)],
        compiler_params=_cparams("parallel", "arbitrary"),
        name="peer_dense",
    )(xn, u_bf16, vt_bf16, rank1, e1, cnt, e0, h)


def _ple_kernel(h_ref, p_ref, g_ref, wg_ref, b_ref, wp_ref, fg_ref, out_ref, *, final):
    h = h_ref[...]
    gate = _sigmoid(_dot(_rms(h, g_ref[...]).astype(BF16), wg_ref[...]) + b_ref[...])
    out = h + gate * _dot(p_ref[...].astype(BF16), wp_ref[...])
    out_ref[...] = _rms(out, fg_ref[...]) if final else out


def _ple(h, p, g, wg_bf16, b, wp_bf16, final_g, final, tm=512):
    t = h.shape[0]
    row = lambda i: (i, 0)
    vec = _full((1, D_MODEL))
    return pl.pallas_call(
        functools.partial(_ple_kernel, final=final),
        grid=(t // tm,),
        in_specs=[pl.BlockSpec((tm, D_MODEL), row), pl.BlockSpec((tm, PLE_DIM), row), vec,
                  _full(wg_bf16.shape), vec, _full(wp_bf16.shape), vec],
        out_specs=pl.BlockSpec((tm, D_MODEL), row),
        out_shape=jax.ShapeDtypeStruct((t, D_MODEL), F32),
        compiler_params=_cparams("parallel"),
        name="ple_final" if final else "ple",
    )(h, p, g, wg_bf16, b, wp_bf16, final_g)


def _mixer(h, bsz, seq, lw, consts):
    qs, ks, vs, u = _inproj(h, lw["norm_mix_g"], lw["w_in"], consts["cos"], consts["sin"], seq)
    att_parts = [_attn_pattern(q, k, v, bsz, seq, dil) for q, k, v, dil in zip(qs, ks, vs, DILATIONS)]
    prep = _rwkv_prep(u, bsz, seq, lw["rwkv_shift"], lw["rwkv_w0"], lw["rwkv_w2"], lw["rwkv_a0"],
                      lw["rwkv_a2"], lw["rwkv_g2"], lw["rwkv_k_k"], lw["rwkv_k_a"], lw["rwkv_r_k"], consts)
    dir_ops, (rv, wdec, bonus, g) = prep[:N_DIR_OPERANDS], prep[N_DIR_OPERANDS:]
    yf, yb = _rwkv_scan(dir_ops, rv, wdec, bsz, seq, consts["scan_masks"])
    return _outproj(h, att_parts, lw["att_norm_g"], yf, yb, bonus, g, lw["rwkv_ln_w"], lw["rwkv_ln_b"],
                    consts["bd512"], lw["w_out"])


def _peer(h, lw):
    xn, rank1, e1, cnt, e0 = _peer_route(h, lw["norm_ffn_g"], lw["peer_w_q_t"], lw["peer_sub_keys"])
    return _peer_dense(h, xn, lw["peer_u"], lw["peer_v_t"], rank1, e1, cnt, e0)


def _layer_weights(i, wts):
    row = lambda a: a[i].reshape(1, -1)
    bf = lambda a: a[i].astype(BF16)
    return dict(
        norm_mix_g=row(wts["norm_mix_g"]), w_in=bf(wts["w_in"]), att_norm_g=row(wts["att_norm_g"]),
        rwkv_shift=wts["rwkv_shift"][i], rwkv_w0=wts["rwkv_w0"][i], rwkv_w2=bf(wts["rwkv_w2"]),
        rwkv_a0=wts["rwkv_a0"][i], rwkv_a2=bf(wts["rwkv_a2"]), rwkv_g2=bf(wts["rwkv_g2"]),
        rwkv_k_k=row(wts["rwkv_k_k"]), rwkv_k_a=row(wts["rwkv_k_a"]), rwkv_r_k=row(wts["rwkv_r_k"]),
        rwkv_ln_w=row(wts["rwkv_ln_w"]), rwkv_ln_b=row(wts["rwkv_ln_b"]), w_out=bf(wts["w_out"]),
        norm_ffn_g=row(wts["norm_ffn_g"]), peer_w_q_t=bf(wts["peer_w_q"]).T,
        peer_sub_keys=bf(wts["peer_sub_keys"]).reshape(2 * PEER_HEADS, PEER_N_KEYS, PEER_N_KEYS),
        peer_u=bf(wts["peer_u"]), peer_v_t=bf(wts["peer_v"]).T,
        ple_norm_g=row(wts["ple_norm_g"]), ple_w_gate=bf(wts["ple_w_gate"]), ple_b_gate=row(wts["ple_b_gate"]),
        ple_w_proj=bf(wts["ple_w_proj"]),
    )


def _trunk(x, p, wts, final_g):
    bsz, seq, _ = x.shape
    depth = p.shape[0]
    t = bsz * seq
    consts = _prep_consts()
    consts["cos"], consts["sin"] = _rope_tables(seq)
    consts["scan_masks"] = _scan_masks()
    h = x.reshape(t, D_MODEL)
    fg = final_g.reshape(1, D_MODEL)
    for i in range(depth):
        lw = _layer_weights(i, wts)
        h = _mixer(h, bsz, seq, lw, consts)
        h = _peer(h, lw)
        h = _ple(h, p[i].reshape(t, PLE_DIM), lw["ple_norm_g"], lw["ple_w_gate"], lw["ple_b_gate"],
                 lw["ple_w_proj"], fg, final=(i == depth - 1))
    return h.reshape(bsz, seq, D_MODEL)


def kernel(x_prompt, x_sample, p_prompt, p_sample, norm_mix_g, w_in, att_norm_g, rwkv_shift, rwkv_w0, rwkv_w2, rwkv_a0, rwkv_a2, rwkv_g2, rwkv_k_k, rwkv_k_a, rwkv_r_k, rwkv_ln_w, rwkv_ln_b, w_out, norm_ffn_g, peer_w_q, peer_sub_keys, peer_u, peer_v, ple_norm_g, ple_w_gate, ple_b_gate, ple_w_proj, final_norm_g):
    wts = dict(norm_mix_g=norm_mix_g, w_in=w_in, att_norm_g=att_norm_g, rwkv_shift=rwkv_shift, rwkv_w0=rwkv_w0,
               rwkv_w2=rwkv_w2, rwkv_a0=rwkv_a0, rwkv_a2=rwkv_a2, rwkv_g2=rwkv_g2, rwkv_k_k=rwkv_k_k,
               rwkv_k_a=rwkv_k_a, rwkv_r_k=rwkv_r_k, rwkv_ln_w=rwkv_ln_w, rwkv_ln_b=rwkv_ln_b, w_out=w_out,
               norm_ffn_g=norm_ffn_g, peer_w_q=peer_w_q, peer_sub_keys=peer_sub_keys, peer_u=peer_u,
               peer_v=peer_v, ple_norm_g=ple_norm_g, ple_w_gate=ple_w_gate, ple_b_gate=ple_b_gate,
               ple_w_proj=ple_w_proj)
    nb = x_prompt.shape[0]
    x = jnp.concatenate([x_prompt, x_sample], axis=0)
    p = jnp.concatenate([p_prompt, p_sample], axis=1)
    y = _trunk(x, p, wts, final_norm_g)
    return (y[:nb], y[nb:])
```

```python
import functools

import jax
import jax.numpy as jnp
import numpy as np
from jax import lax
from jax.experimental import pallas as pl
from jax.experimental.pallas import tpu as pltpu

F32 = jnp.float32
BF16 = jnp.bfloat16

D_MODEL = 1024
HEAD_DIM = 64
ATT_WIDTH = 512
ATT_HEADS = 8
RWKV_WIDTH = 512
RWKV_COLS = 1920
QKV_COLS = 3 * ATT_WIDTH
DILATIONS = (1, 4, 16)
HALF_WINDOW = 64
ROPE_THETA = 500000.0
ROPE_DIM = 16
NORM_EPS = 1e-6
LN_X_EPS = 64e-5
NEG_INF = -1e30
PEER_HEADS = 8
PEER_N_KEYS = 128
PEER_TOPK = 16
PLE_DIM = 256
CHUNK = 64
GROUP = 256

VMEM_LIMIT = 56 * 1024 * 1024


def _cparams(*sem, flags=None):
    return pltpu.CompilerParams(dimension_semantics=sem, vmem_limit_bytes=VMEM_LIMIT, flags=flags)


def _full(shape):
    nd = len(shape)
    return pl.BlockSpec(shape, lambda *_: (0,) * nd)


def _dot(a, b):
    return jnp.dot(a, b, preferred_element_type=F32)


def _dot_nt(a, b):
    return lax.dot_general(a, b, (((1,), (1,)), ((), ())), preferred_element_type=F32)


def _dot_tn(a, b):
    return lax.dot_general(a, b, (((0,), (0,)), ((), ())), preferred_element_type=F32)


def _split3(x):
    hi = x.astype(BF16)
    r1 = x - hi.astype(F32)
    mid = r1.astype(BF16)
    lo = (r1 - mid.astype(F32)).astype(BF16)
    return hi, mid, lo


def _dot_rexact(x, r_bf16):
    hi, mid, lo = _split3(x)
    return _dot(hi, r_bf16) + _dot(mid, r_bf16) + _dot(lo, r_bf16)


def _rms(x, g):
    return x * lax.rsqrt(jnp.mean(x * x, axis=-1, keepdims=True) + NORM_EPS) * g


def _sigmoid(x):
    return 1.0 / (1.0 + jnp.exp(-x))


LANES = 128
ATT_LANE_TILES = ATT_WIDTH // LANES


def _to_classes(t, stage, out_refs):
    rows = t.shape[0]
    for c in range(ATT_LANE_TILES):
        stage[c] = t[:, c * LANES:(c + 1) * LANES]
    for dil, ref in zip(DILATIONS, out_refs):
        if dil == 1:
            ref[...] = t.astype(ref.dtype)
            continue
        for r in range(dil):
            for c in range(ATT_LANE_TILES):
                lo = r * ATT_WIDTH + c * LANES
                ref[:, lo:lo + LANES] = stage[c, pl.ds(r, rows // dil, stride=dil), :].astype(ref.dtype)


def _from_classes(ref, dil, stage):
    if dil == 1:
        return ref[...]
    rows = ref.shape[0] * dil
    for r in range(dil):
        for c in range(ATT_LANE_TILES):
            lo = r * ATT_WIDTH + c * LANES
            stage[c, pl.ds(r, rows // dil, stride=dil), :] = ref[:, lo:lo + LANES]
    return jnp.concatenate([stage[c] for c in range(ATT_LANE_TILES)], axis=1)


def _inproj_kernel(h_ref, g_ref, w_ref, cos_ref, sin_ref, *refs):
    nd = len(DILATIONS)
    q_refs, k_refs, v_refs = refs[:nd], refs[nd:2 * nd], refs[2 * nd:3 * nd]
    u_ref, stage = refs[3 * nd:]
    xn = _rms(h_ref[...], g_ref[...]).astype(BF16)
    qkv = _dot(xn, w_ref[:, :QKV_COLS])
    u_ref[...] = _dot(xn, w_ref[:, QKV_COLS:])
    cos = cos_ref[...]
    sin = sin_ref[...]
    lane = lax.broadcasted_iota(jnp.int32, cos.shape, 1) % HEAD_DIM
    half = ROPE_DIM // 2

    def rot(t):
        partner = jnp.where(lane < half, pltpu.roll(t, ATT_WIDTH - half, 1), pltpu.roll(t, half, 1))
        return t * cos + partner * sin

    _to_classes(rot(qkv[:, :ATT_WIDTH]) * (HEAD_DIM ** -0.5), stage, q_refs)
    _to_classes(rot(qkv[:, ATT_WIDTH:2 * ATT_WIDTH]), stage, k_refs)
    _to_classes(qkv[:, 2 * ATT_WIDTH:], stage, v_refs)


def _inproj(h, g, w_bf16, cos_t, sin_t, seq, tm=512):
    t = h.shape[0]
    tm = min(tm, seq)
    nseq = seq // tm
    row = lambda i: (i, 0)
    pos = lambda i: (i % nseq, 0)
    cls_specs = [pl.BlockSpec((tm // d, d * ATT_WIDTH), row) for d in DILATIONS]
    cls_shapes = [jax.ShapeDtypeStruct((t // d, d * ATT_WIDTH), BF16) for d in DILATIONS]
    outs = pl.pallas_call(
        _inproj_kernel,
        grid=(t // tm,),
        in_specs=[pl.BlockSpec((tm, D_MODEL), row), _full((1, D_MODEL)), _full(w_bf16.shape),
                  pl.BlockSpec((tm, ATT_WIDTH), pos), pl.BlockSpec((tm, ATT_WIDTH), pos)],
        out_specs=cls_specs * 3 + [pl.BlockSpec((tm, RWKV_COLS), row)],
        out_shape=cls_shapes * 3 + [jax.ShapeDtypeStruct((t, RWKV_COLS), F32)],
        scratch_shapes=[pltpu.VMEM((ATT_LANE_TILES, tm, LANES), F32)],
        compiler_params=_cparams("parallel"),
        name="inproj",
    )(h, g, w_bf16, cos_t, sin_t)
    nd = len(DILATIONS)
    return outs[:nd], outs[nd:2 * nd], outs[2 * nd:3 * nd], outs[3 * nd]


def _rope_tables(seq):
    half = ROPE_DIM // 2
    inv_freq = ROPE_THETA ** (-jnp.arange(half, dtype=F32) / half)
    ang = jnp.arange(seq, dtype=F32)[:, None] * inv_freq[None, :]
    ones = jnp.ones((seq, HEAD_DIM - ROPE_DIM), F32)
    cos_h = jnp.concatenate([jnp.cos(ang), jnp.cos(ang), ones], axis=1)
    sin_h = jnp.concatenate([-jnp.sin(ang), jnp.sin(ang), 0.0 * ones], axis=1)
    return jnp.tile(cos_h, (1, ATT_HEADS)), jnp.tile(sin_h, (1, ATT_HEADS))


ATT_SUB = 128


def _attn_kernel(q_ref, kp_ref, kc_ref, kn_ref, vp_ref, vc_ref, vn_ref, o_ref, lse_ref,
                 kbuf, vbuf, *, length, tq):
    j = pl.program_id(2)
    hw = HALF_WINDOW
    kbuf[0:hw] = kp_ref[0]
    kbuf[hw:hw + tq] = kc_ref[0]
    kbuf[hw + tq:] = kn_ref[0]
    vbuf[0:hw] = vp_ref[0]
    vbuf[hw:hw + tq] = vc_ref[0]
    vbuf[hw + tq:] = vn_ref[0]
    nk = ATT_SUB + 2 * hw
    qi = lax.broadcasted_iota(jnp.int32, (ATT_SUB, nk), 0)
    ki = lax.broadcasted_iota(jnp.int32, (ATT_SUB, nk), 1)
    band = jnp.abs(ki - hw - qi) <= hw
    heads = [slice(h * HEAD_DIM, (h + 1) * HEAD_DIM) for h in range(ATT_HEADS)]
    blocks = [(s * ATT_SUB, c) for s in range(tq // ATT_SUB) for c in heads]
    sc = jnp.concatenate([_dot_nt(q_ref[0, q0:q0 + ATT_SUB, c], kbuf[q0:q0 + nk, c]) for q0, c in blocks], axis=0)
    valid = []
    for s in range(tq // ATT_SUB):
        kpos = j * tq + (s * ATT_SUB - hw) + ki
        valid += [band & (kpos >= 0) & (kpos < length)] * ATT_HEADS
    sc = jnp.where(jnp.concatenate(valid, axis=0), sc, NEG_INF)
    m = jnp.max(sc, axis=-1, keepdims=True)
    p = jnp.exp(sc - m)
    l = jnp.sum(p, axis=-1, keepdims=True)
    pb = p.astype(BF16)
    inv_l = 1.0 / l
    lse = m + jnp.log(l)
    for n, (q0, c) in enumerate(blocks):
        rows = slice(n * ATT_SUB, (n + 1) * ATT_SUB)
        o_ref[0, q0:q0 + ATT_SUB, c] = _dot(pb[rows], vbuf[q0:q0 + nk, c]) * inv_l[rows]
        lse_ref[0, q0:q0 + ATT_SUB, c] = jnp.broadcast_to(lse[rows], (ATT_SUB, HEAD_DIM))


def _attn_pattern(q, k, v, bsz, seq, dil):
    length = seq // dil
    tq = min(length, 512)
    nq = length // tq
    nh = length // HALF_WINDOW
    rq = tq // HALF_WINDOW
    view = lambda a: a.reshape(bsz, length, dil * ATT_WIDTH)
    cur = pl.BlockSpec((1, tq, ATT_WIDTH), lambda b, r, j: (b, j, r))
    prev = pl.BlockSpec((1, HALF_WINDOW, ATT_WIDTH), lambda b, r, j: (b, jnp.maximum(j * rq - 1, 0), r))
    nxt = pl.BlockSpec((1, HALF_WINDOW, ATT_WIDTH), lambda b, r, j: (b, jnp.minimum((j + 1) * rq, nh - 1), r))
    out_sd = jax.ShapeDtypeStruct((bsz, length, dil * ATT_WIDTH), F32)
    o, lse = pl.pallas_call(
        functools.partial(_attn_kernel, length=length, tq=tq),
        grid=(bsz, dil, nq),
        in_specs=[cur, prev, cur, nxt, prev, cur, nxt],
        out_specs=[cur, cur],
        out_shape=[out_sd, out_sd],
        scratch_shapes=[pltpu.VMEM((tq + 2 * HALF_WINDOW, ATT_WIDTH), BF16)] * 2,
        compiler_params=_cparams("parallel", "parallel", "parallel"),
        name=f"attn_d{dil}",
    )(view(q), view(k), view(k), view(k), view(v), view(v), view(v))
    flat = (bsz * length, dil * ATT_WIDTH)
    return o.reshape(flat), lse.reshape(flat)


R_TILE = 256
R_COLS = dict(r=0, k=512, v=1024, wd=1536, ad=1664, gd=1792)
DECAY_LOG_SCALE = float(np.exp(-0.5))


def _rwkv_prep_kernel(uc_ref, up_ref, un_ref, shift_ref, w0_ref, w2_ref, a0_ref, a2_ref, g2_ref,
                      kk_ref, ka_ref, rk_ref, bd_ref, tri_ref, sel_ref,
                      rt_ref, at_ref, bt_ref, kt_ref, bh_ref, kh_ref, v_ref, wdec_ref, bonus_ref, g_ref):
    j = pl.program_id(1)
    nj = pl.num_programs(1)
    tr = R_TILE
    cur = uc_ref[0]
    row = lax.broadcasted_iota(jnp.int32, cur.shape, 0)
    prev_edge = up_ref[0, 7:8, :] * (j > 0).astype(F32)
    next_edge = un_ref[0, 0:1, :] * (j < nj - 1).astype(F32)
    prev = jnp.where(row == 0, prev_edge, pltpu.roll(cur, 1, 0))
    nxt = jnp.where(row == tr - 1, next_edge, pltpu.roll(cur, tr - 1, 0))
    us = shift_ref[0:1, :] * prev + shift_ref[1:2, :] * cur + shift_ref[2:3, :] * nxt

    w = RWKV_WIDTH
    r = us[:, R_COLS["r"]:R_COLS["r"] + w]
    k = us[:, R_COLS["k"]:R_COLS["k"] + w]
    v = us[:, R_COLS["v"]:R_COLS["v"] + w]
    gd = us[:, R_COLS["gd"]:]
    bd = bd_ref[...]

    g_ref[...] = _dot(_sigmoid(gd).astype(BF16), g2_ref[...])
    v_ref[...] = v.astype(BF16)
    kk = k * kk_ref[...]
    kk = kk / jnp.maximum(jnp.sqrt(_dot_rexact(kk * kk, bd)), 1e-12)

    nchunk = tr // CHUNK
    ksum = jnp.zeros_like(k)
    for d in range(2):
        wd = us[:, R_COLS["wd"] + 64 * d:R_COLS["wd"] + 64 * (d + 1)]
        ad = us[:, R_COLS["ad"] + 64 * d:R_COLS["ad"] + 64 * (d + 1)]
        z = w0_ref[d:d + 1, :] + _dot(jnp.tanh(wd).astype(BF16), w2_ref[d])
        lw = -DECAY_LOG_SCALE * _sigmoid(z)
        a = _sigmoid(a0_ref[d:d + 1, :] + _dot(ad.astype(BF16), a2_ref[d]))
        kd = k * (1.0 + (a - 1.0) * ka_ref[...])
        ksum = ksum + kd
        b = kk * a
        lw3 = _split3(lw)
        sum3 = lambda l: _dot(l, lw3[0]) + _dot(l, lw3[1]) + _dot(l, lw3[2])
        c = sum3(tri_ref[d])
        tot = sum3(sel_ref[...])
        cend = jnp.concatenate(
            [jnp.broadcast_to(tot[i:i + 1, :], (CHUNK, w)) for i in range(nchunk)], axis=0)
        e_c = jnp.exp(c)
        e_nc = jnp.exp(-c)
        e_rem = jnp.exp(cend - c)
        rt_ref[d] = (r * e_c).astype(BF16)
        at_ref[d] = (-kk * jnp.exp(c - lw)).astype(BF16)
        bt_ref[d] = (b * e_nc).astype(BF16)
        kt_ref[d] = (kd * e_nc).astype(BF16)
        bh_ref[d] = (b * e_rem).astype(BF16)
        kh_ref[d] = (kd * e_rem).astype(BF16)
        wdec = jnp.exp(tot)
        for i in range(nchunk):
            wdec_ref[d, 0, i] = wdec[i:i + 1, :]
    bonus_ref[...] = _dot_rexact(r * ksum * rk_ref[...], bd) * v


def _prep_consts():
    tr = R_TILE
    idx = np.arange(tr)
    same = (idx[:, None] // CHUNK) == (idx[None, :] // CHUNK)
    tri = np.stack([same & (idx[:, None] >= idx[None, :]), same & (idx[:, None] <= idx[None, :])])
    sel = (np.arange(8)[:, None] == (idx[None, :] // CHUNK))
    hd = np.arange(RWKV_WIDTH) // HEAD_DIM
    bd = hd[:, None] == hd[None, :]
    as_bf16 = lambda m: jnp.asarray(m.astype(np.float32), dtype=BF16)
    return dict(tri=as_bf16(tri), sel=as_bf16(sel), bd512=as_bf16(bd))


def _rwkv_prep(u, bsz, seq, shift_w, w0, w2, a0, a2, g2, k_k, k_a, r_k, consts):
    tr = R_TILE
    nj = seq // tr
    t = bsz * seq
    w = RWKV_WIDTH
    u3 = u.reshape(bsz, seq, RWKV_COLS)
    rows8 = tr // 8
    cur = pl.BlockSpec((1, tr, RWKV_COLS), lambda b, j: (b, j, 0))
    prev = pl.BlockSpec((1, 8, RWKV_COLS), lambda b, j: (b, jnp.maximum(j * rows8 - 1, 0), 0))
    nxt = pl.BlockSpec((1, 8, RWKV_COLS), lambda b, j: (b, jnp.minimum((j + 1) * rows8, seq // 8 - 1), 0))
    tok2 = pl.BlockSpec((2, tr, w), lambda b, j: (0, b * nj + j, 0))
    tok = pl.BlockSpec((tr, w), lambda b, j: (b * nj + j, 0))
    nchunk = tr // CHUNK
    wdec_spec = pl.BlockSpec((2, 1, nchunk, 1, w), lambda b, j: (0, b, j, 0, 0))
    small = [shift_w, w0, w2, a0, a2, g2, k_k, k_a, r_k, consts["bd512"], consts["tri"], consts["sel"]]
    sd2 = jax.ShapeDtypeStruct((2, t, w), BF16)
    return pl.pallas_call(
        _rwkv_prep_kernel,
        grid=(bsz, nj),
        in_specs=[cur, prev, nxt] + [_full(a.shape) for a in small],
        out_specs=[tok2] * 6 + [tok, wdec_spec, tok, tok],
        out_shape=[sd2] * 6 + [jax.ShapeDtypeStruct((t, w), BF16),
                               jax.ShapeDtypeStruct((2, bsz, seq // CHUNK, 1, w), F32),
                               jax.ShapeDtypeStruct((t, w), F32), jax.ShapeDtypeStruct((t, w), F32)],
        compiler_params=_cparams("parallel", "parallel"),
        name="rwkv_prep",
    )(u3, u3, u3, *small)


S_TILE = 128
LEVELS = (2, 4, 8, 16, 32)
N_DIR_OPERANDS = 6


def _rwkv_scan_kernel(*refs):
    nop = N_DIR_OPERANDS
    dir_refs = refs[:2 * nop]
    vf_ref, vb_ref, wf_ref, wb_ref, mask_ref, yf_ref, yb_ref, state = refs[2 * nop:]
    j = pl.program_id(1)

    @pl.when(j == 0)
    def _():
        state[...] = jnp.zeros_like(state)

    nchunk = S_TILE // CHUNK
    nh = GROUP // HEAD_DIM
    ng = RWKV_WIDTH // GROUP
    bdm = mask_ref[0]
    bdm_f32 = bdm.astype(F32)
    eye = mask_ref[1].astype(F32)
    tile4 = lambda x: jnp.concatenate([x] * nh, axis=0)
    io = ((vf_ref, wf_ref, yf_ref), (vb_ref, wb_ref, yb_ref))
    order = [(d, g, ci if d == 0 else nchunk - 1 - ci)
             for ci in range(nchunk) for g in range(ng) for d in range(2)]

    pre = {}
    for d, g, c in order:
        rt_ref, at_ref, bt_ref, kt_ref, bh_ref, kh_ref = [dir_refs[2 * i + d] for i in range(nop)]
        rows = (0, slice(c * CHUNK, (c + 1) * CHUNK), slice(g * GROUP, (g + 1) * GROUP))
        a_bd = tile4(at_ref[rows]) * bdm
        r_bd = tile4(rt_ref[rows]) * bdm
        v_bd = tile4(io[d][0][rows]) * bdm
        ar = jnp.concatenate([a_bd, r_bd], axis=0)
        ar_b = _dot_nt(ar, tile4(bt_ref[rows]))
        ar_k = _dot_nt(ar, tile4(kt_ref[rows]))
        strict = mask_ref[2 + d]
        incl = mask_ref[4 + d]
        a_ab = ar_b[:GROUP].astype(BF16) * strict
        a_ak = ar_k[:GROUP].astype(BF16) * strict
        a_r = jnp.concatenate([ar_b[GROUP:].astype(BF16) * incl, ar_k[GROUP:].astype(BF16) * incl], axis=1)
        hat = jnp.concatenate([tile4(bh_ref[rows]), tile4(kh_ref[rows])], axis=0)
        pre[d, g, c] = dict(a_bd=a_bd, r_bd=r_bd, v_bd=v_bd, a_ab=a_ab, a_ak=a_ak, a_r=a_r, hat=hat,
                            tinv=eye + (a_ab * mask_ref[6 + d]).astype(F32))
    for li in range(len(LEVELS)):
        for key in order:
            p = pre[key]
            lm = p["a_ab"] * mask_ref[8 + 2 * li + key[0]]
            tb = p["tinv"].astype(BF16)
            p["tinv"] = p["tinv"] + _dot(tb, _dot(lm, tb).astype(BF16))
    s = {(d, g): state[d, g] for d in range(2) for g in range(ng)}
    for d, g, c in order:
        p = pre[d, g, c]
        w_ref, y_ref = io[d][1], io[d][2]
        cols = slice(g * GROUP, (g + 1) * GROUP)
        s_bf = s[d, g].astype(BF16)
        rhs = _dot_nt(p["a_bd"], s_bf) + _dot(p["a_ak"], p["v_bd"])
        u_bf = _dot(p["tinv"].astype(BF16), rhs.astype(BF16)).astype(BF16)
        uv = jnp.concatenate([u_bf, p["v_bd"]], axis=0)
        y = _dot_nt(p["r_bd"], s_bf) + _dot(p["a_r"], uv)
        y_nat = y[0:CHUNK]
        for hh in range(1, nh):
            y_nat = y_nat + y[hh * CHUNK:(hh + 1) * CHUNK]
        y_ref[0, c * CHUNK:(c + 1) * CHUNK, cols] = y_nat
        s[d, g] = s[d, g] * w_ref[0, 0, c, :, cols] + _dot_tn(uv, p["hat"]) * bdm_f32
    for d in range(2):
        for g in range(ng):
            state[d, g] = s[d, g]


def _scan_masks():
    n = GROUP
    idx = np.arange(n)
    head = idx // CHUNK
    t = idx % CHUNK
    same = head[:, None] == head[None, :]
    tt, ss = t[:, None], t[None, :]
    masks = [same, np.eye(n, dtype=bool)]
    masks += [same & (tt > ss), same & (tt < ss)]
    masks += [same & (tt >= ss), same & (tt <= ss)]
    for m in (1,) + LEVELS:
        blk = same & (tt // (2 * m) == ss // (2 * m))
        lower = blk & ((tt // m) % 2 == 1) & ((ss // m) % 2 == 0)
        upper = blk & ((tt // m) % 2 == 0) & ((ss // m) % 2 == 1)
        masks += [lower, upper]
    return jnp.asarray(np.stack(masks).astype(np.float32), dtype=BF16)


def _rwkv_scan(dir_ops, v, wdec, bsz, seq, masks):
    st = S_TILE
    nj = seq // st
    ng = RWKV_WIDTH // GROUP
    w = RWKV_WIDTH
    t = bsz * seq
    nchunk = st // CHUNK
    blk = (1, st, w)
    fwd = lambda d: pl.BlockSpec(blk, lambda b, j: (d, b * nj + j, 0))
    bwd = lambda d: pl.BlockSpec(blk, lambda b, j: (d, b * nj + (nj - 1 - j), 0))
    wf = pl.BlockSpec((1, 1, nchunk, 1, w), lambda b, j: (0, b, j, 0, 0))
    wb = pl.BlockSpec((1, 1, nchunk, 1, w), lambda b, j: (1, b, nj - 1 - j, 0, 0))
    v3 = v.reshape(1, t, w)
    in_specs, args = [], []
    for arr in dir_ops:
        in_specs += [fwd(0), bwd(1)]
        args += [arr, arr]
    in_specs += [fwd(0), bwd(0), wf, wb, _full(masks.shape)]
    args += [v3, v3, wdec, wdec, masks]
    sd = jax.ShapeDtypeStruct((1, t, w), F32)
    yf, yb = pl.pallas_call(
        _rwkv_scan_kernel,
        grid=(bsz, nj),
        in_specs=in_specs,
        out_specs=[fwd(0), bwd(0)],
        out_shape=[sd, sd],
        scratch_shapes=[pltpu.VMEM((2, ng, GROUP, GROUP), F32)],
        compiler_params=_cparams("parallel", "arbitrary"),
        name="rwkv_scan",
    )(*args)
    return yf.reshape(t, w), yb.reshape(t, w)


def _outproj_kernel(h_ref, o1_ref, o2_ref, o3_ref, l1_ref, l2_ref, l3_ref, attg_ref,
                    yf_ref, yb_ref, bonus_ref, g_ref, lnw_ref, lnb_ref, bd_ref, w_ref, out_ref, stage):
    bd = bd_ref[...]
    inv_n = 1.0 / HEAD_DIM
    nd = len(DILATIONS)
    o1, o2, o3 = [_from_classes(r, d, stage.at[i])
                  for i, (r, d) in enumerate(zip((o1_ref, o2_ref, o3_ref), DILATIONS))]
    l1, l2, l3 = [_from_classes(r, d, stage.at[nd + i])
                  for i, (r, d) in enumerate(zip((l1_ref, l2_ref, l3_ref), DILATIONS))]
    m = jnp.maximum(jnp.maximum(l1, l2), l3)
    w1, w2, w3 = jnp.exp(l1 - m), jnp.exp(l2 - m), jnp.exp(l3 - m)
    o = (w1 * o1 + w2 * o2 + w3 * o3) / (w1 + w2 + w3)
    att = o * lax.rsqrt(_dot_rexact(o * o, bd) * inv_n + NORM_EPS) * attg_ref[...]
    y = yf_ref[...] + yb_ref[...]
    yc = y - _dot_rexact(y, bd) * inv_n
    var = _dot_rexact(yc * yc, bd) * inv_n
    yn = yc * lax.rsqrt(var + LN_X_EPS) * lnw_ref[...] + lnb_ref[...]
    rw = (yn + bonus_ref[...]) * g_ref[...]
    out_ref[...] = (h_ref[...] + _dot(att.astype(BF16), w_ref[:ATT_WIDTH, :])
                    + _dot(rw.astype(BF16), w_ref[ATT_WIDTH:, :]))


def _outproj(h, att_parts, att_g, yf, yb, bonus, g, ln_w, ln_b, bd512, w_out_bf16, tm=256):
    t = h.shape[0]
    row = lambda i: (i, 0)
    half = pl.BlockSpec((tm, ATT_WIDTH), row)
    cls = [pl.BlockSpec((tm // d, d * ATT_WIDTH), row) for d in DILATIONS]
    (o1, l1), (o2, l2), (o3, l3) = att_parts
    vec = _full((1, ATT_WIDTH))
    return pl.pallas_call(
        _outproj_kernel,
        grid=(t // tm,),
        in_specs=[pl.BlockSpec((tm, D_MODEL), row)] + cls * 2 + [vec] + [half] * 4 + [vec, vec,
                  _full(bd512.shape), _full(w_out_bf16.shape)],
        out_specs=pl.BlockSpec((tm, D_MODEL), row),
        out_shape=jax.ShapeDtypeStruct((t, D_MODEL), F32),
        scratch_shapes=[pltpu.VMEM((2 * len(DILATIONS), ATT_LANE_TILES, tm, LANES), F32)],
        compiler_params=_cparams("parallel"),
        name="outproj",
    )(h, o1, o2, o3, l1, l2, l3, att_g, yf, yb, bonus, g, ln_w, ln_b, bd512, w_out_bf16)


ROUTE_TILE = 256
ROUTE_HEADS_PER_TRIP = 8


def _oddeven_merge_sort(n):
    pairs = []
    p = 1
    while p < n:
        k = p
        while k >= 1:
            for j in range(k % p, n - k, 2 * k):
                for i in range(min(k, n - j - k)):
                    if (i + j) // (2 * p) == (i + j + k) // (2 * p):
                        pairs.append((i + j, i + j + k))
            k //= 2
        p *= 2
    return tuple(pairs)


SUBLANES = 8


def _top_values(s, k, on_round):
    nv = s.shape[0] // SUBLANES
    x = [s[SUBLANES * v:SUBLANES * (v + 1), :] for v in range(nv)]
    for a, b in _oddeven_merge_sort(nv):
        x[a], x[b] = jnp.maximum(x[a], x[b]), jnp.minimum(x[a], x[b])
    for r in range(k):
        m = jnp.max(x[0], axis=0, keepdims=True)
        on_round(r, m)
        if r + 1 < k:
            hit = x[0] == m
            depth = min(nv - 1, k - 1 - r)
            for v in range(depth):
                x[v] = jnp.where(hit, x[v + 1], x[v])
            x[depth] = jnp.where(hit, NEG_INF, x[depth])


def _peer_route_kernel(h_ref, g_ref, wq_ref, sk_ref, xn_ref, rank1_ref, e1_ref, cnt_ref, e0_ref, qt):
    tp = ROUTE_TILE
    nk = PEER_N_KEYS
    k = PEER_TOPK
    xn = _rms(h_ref[...], g_ref[...]).astype(BF16)
    xn_ref[...] = xn
    qt[...] = _dot_nt(wq_ref[...], xn)
    row16 = lax.broadcasted_iota(jnp.int32, (k, tp), 0)

    def one_head(h):
        tops, scores = [], []
        for p in range(2):
            hp = 2 * h + p
            q_hp = qt[pl.ds(pl.multiple_of(hp * nk, nk), nk), :].astype(BF16)
            s = _dot(sk_ref[hp], q_hp)
            scores.append(s)
            box = [jnp.full((k, tp), NEG_INF, F32)]

            def on_round(r, m, box=box):
                box[0] = jnp.where(row16 == r, m, box[0])

            _top_values(s, k, on_round)
            tops.append(box[0])
        t0, t1 = tops
        half = k // 2
        cand = jnp.concatenate([t0[0:1, :] + t1] + [t0[a:a + 1, :] + t1[0:half, :] for a in range(1, half)]
                               + [t0[half:, :] + t1[b:b + 1, :] for b in range(half - 1)], axis=0)
        top = t0[0:1, :] + t1[0:1, :]
        acc = [jnp.zeros((1, tp), F32), top]

        def on_cand(r, m, acc=acc):
            acc[0] = acc[0] + jnp.exp(m - top)
            acc[1] = m

        _top_values(cand, k, on_cand)
        z, thr = acc
        n = jnp.zeros((k, tp), F32)
        for b in range(k):
            n = n + jnp.where(t0 + t1[b:b + 1, :] >= thr, 1.0, 0.0)
        cnt = jnp.zeros((nk, tp), F32)
        for a in range(k):
            cnt = jnp.where(scores[0] == t0[a:a + 1, :], n[a:a + 1, :], cnt)
        rank1 = jnp.full((nk, tp), float(k), F32)
        for b in reversed(range(k)):
            rank1 = jnp.where(scores[1] == t1[b:b + 1, :], float(b), rank1)
        rank1_ref[h] = rank1.astype(BF16)
        e1_ref[h] = jnp.exp(scores[1] - t1[0:1, :]).astype(BF16)
        cnt_ref[h] = cnt
        e0_ref[h] = jnp.exp(scores[0] - t0[0:1, :]) / z

    def head_group(i, carry):
        for hh in range(ROUTE_HEADS_PER_TRIP):
            one_head(ROUTE_HEADS_PER_TRIP * i + hh)
        return carry

    lax.fori_loop(0, PEER_HEADS // ROUTE_HEADS_PER_TRIP, head_group, 0)


def _peer_route(h, g, wq_t_bf16, sub_keys_bf16):
    t = h.shape[0]
    tp = ROUTE_TILE
    nk = PEER_N_KEYS
    nh = PEER_HEADS
    row = lambda i: (i, 0)
    lanes = lambda i: (0, 0, i)
    spec = pl.BlockSpec((nh, nk, tp), lanes)
    sd = lambda dt: jax.ShapeDtypeStruct((nh, nk, t), dt)
    return pl.pallas_call(
        _peer_route_kernel,
        grid=(t // tp,),
        in_specs=[pl.BlockSpec((tp, D_MODEL), row), _full((1, D_MODEL)), _full(wq_t_bf16.shape),
                  _full(sub_keys_bf16.shape)],
        out_specs=[pl.BlockSpec((tp, D_MODEL), row), spec, spec, spec, spec],
        out_shape=[jax.ShapeDtypeStruct((t, D_MODEL), BF16), sd(BF16), sd(BF16), sd(F32), sd(F32)],
        scratch_shapes=[pltpu.VMEM((nh * 2 * nk, tp), F32)],
        compiler_params=_cparams("parallel"),
        name="peer_route",
    )(h, g, wq_t_bf16, sub_keys_bf16)


E_TILE = 2048
DENSE_TILE = 512


BF16_ROWS = 16


def _rows_bf16(row, nrows):
    one = jnp.broadcast_to(row, (BF16_ROWS, row.shape[1])).astype(BF16)
    return jnp.concatenate([one] * (nrows // BF16_ROWS), axis=0)


def _peer_dense_kernel(xn_ref, u_ref, vt_ref, rank1_ref, e1_ref, cnt_ref, e0_ref, h_ref, out_ref, acc):
    e = pl.program_id(1)
    nk = PEER_N_KEYS
    tp = xn_ref.shape[0]

    @pl.when(e == 0)
    def _():
        acc[...] = jnp.zeros_like(acc)

    gates = []
    for ii in range(E_TILE // nk):
        i = e * (E_TILE // nk) + ii
        gate = jnp.zeros((nk, tp), BF16)
        for h in range(PEER_HEADS):
            cnt_row = _rows_bf16(cnt_ref[h, pl.ds(i, 1), :], nk)
            e0_row = _rows_bf16(e0_ref[h, pl.ds(i, 1), :], nk)
            gate = gate + jnp.where(rank1_ref[h] < cnt_row, e1_ref[h], jnp.zeros((), BF16)) * e0_row
        gates.append(gate)
    hid = _dot_nt(u_ref[...], xn_ref[...])
    hb = hid.astype(BF16)
    act = (0.5 * hb) * (1.0 + lax.erf(hb * (2.0 ** -0.5)))
    acc[...] += _dot(vt_ref[...], act * jnp.concatenate(gates, axis=0))

    @pl.when(e == pl.num_programs(1) - 1)
    def _():
        out_ref[...] = h_ref[...] + acc[...].T


def _peer_dense(h, xn, u_bf16, vt_bf16, rank1, e1, cnt, e0):
    t = h.shape[0]
    tp = min(DENSE_TILE, t)
    ne = u_bf16.shape[0]
    nh, nk = PEER_HEADS, PEER_N_KEYS
    tok = lambda i, e: (i, 0)
    lanes = lambda i, e: (0, 0, i)
    spec = pl.BlockSpec((nh, nk, tp), lanes)
    return pl.pallas_call(
        _peer_dense_kernel,
        grid=(t // tp, ne // E_TILE),
        in_specs=[pl.BlockSpec((tp, D_MODEL), tok), pl.BlockSpec((E_TILE, D_MODEL), lambda i, e: (e, 0)),
                  pl.BlockSpec((D_MODEL, E_TILE), lambda i, e: (0, e)), spec, spec, spec, spec,
                  pl.BlockSpec((tp, D_MODEL), tok)],
        out_specs=pl.BlockSpec((tp, D_MODEL), tok),
        out_shape=jax.ShapeDtypeStruct((t, D_MODEL), F32),
        scratch_shapes=[pltpu.VMEM((D_MODEL, tp), F32)],
        compiler_params=_cparams("parallel", "arbitrary"),
        name="peer_dense",
    )(xn, u_bf16, vt_bf16, rank1, e1, cnt, e0, h)


def _ple_kernel(h_ref, p_ref, g_ref, wg_ref, b_ref, wp_ref, fg_ref, out_ref, *, final):
    h = h_ref[...]
    gate = _sigmoid(_dot(_rms(h, g_ref[...]).astype(BF16), wg_ref[...]) + b_ref[...])
    out = h + gate * _dot(p_ref[...].astype(BF16), wp_ref[...])
    out_ref[...] = _rms(out, fg_ref[...]) if final else out


def _ple(h, p, g, wg_bf16, b, wp_bf16, final_g, final, tm=512):
    t = h.shape[0]
    row = lambda i: (i, 0)
    vec = _full((1, D_MODEL))
    return pl.pallas_call(
        functools.partial(_ple_kernel, final=final),
        grid=(t // tm,),
        in_specs=[pl.BlockSpec((tm, D_MODEL), row), pl.BlockSpec((tm, PLE_DIM), row), vec,
                  _full(wg_bf16.shape), vec, _full(wp_bf16.shape), vec],
        out_specs=pl.BlockSpec((tm, D_MODEL), row),
        out_shape=jax.ShapeDtypeStruct((t, D_MODEL), F32),
        compiler_params=_cparams("parallel"),
        name="ple_final" if final else "ple",
    )(h, p, g, wg_bf16, b, wp_bf16, final_g)


def _mixer(h, bsz, seq, lw, consts):
    qs, ks, vs, u = _inproj(h, lw["norm_mix_g"], lw["w_in"], consts["cos"], consts["sin"], seq)
    att_parts = [_attn_pattern(q, k, v, bsz, seq, dil) for q, k, v, dil in zip(qs, ks, vs, DILATIONS)]
    prep = _rwkv_prep(u, bsz, seq, lw["rwkv_shift"], lw["rwkv_w0"], lw["rwkv_w2"], lw["rwkv_a0"],
                      lw["rwkv_a2"], lw["rwkv_g2"], lw["rwkv_k_k"], lw["rwkv_k_a"], lw["rwkv_r_k"], consts)
    dir_ops, (rv, wdec, bonus, g) = prep[:N_DIR_OPERANDS], prep[N_DIR_OPERANDS:]
    yf, yb = _rwkv_scan(dir_ops, rv, wdec, bsz, seq, consts["scan_masks"])
    return _outproj(h, att_parts, lw["att_norm_g"], yf, yb, bonus, g, lw["rwkv_ln_w"], lw["rwkv_ln_b"],
                    consts["bd512"], lw["w_out"])


def _peer(h, lw):
    xn, rank1, e1, cnt, e0 = _peer_route(h, lw["norm_ffn_g"], lw["peer_w_q_t"], lw["peer_sub_keys"])
    return _peer_dense(h, xn, lw["peer_u"], lw["peer_v_t"], rank1, e1, cnt, e0)


def _layer_weights(i, wts):
    row = lambda a: a[i].reshape(1, -1)
    bf = lambda a: a[i].astype(BF16)
    return dict(
        norm_mix_g=row(wts["norm_mix_g"]), w_in=bf(wts["w_in"]), att_norm_g=row(wts["att_norm_g"]),
        rwkv_shift=wts["rwkv_shift"][i], rwkv_w0=wts["rwkv_w0"][i], rwkv_w2=bf(wts["rwkv_w2"]),
        rwkv_a0=wts["rwkv_a0"][i], rwkv_a2=bf(wts["rwkv_a2"]), rwkv_g2=bf(wts["rwkv_g2"]),
        rwkv_k_k=row(wts["rwkv_k_k"]), rwkv_k_a=row(wts["rwkv_k_a"]), rwkv_r_k=row(wts["rwkv_r_k"]),
        rwkv_ln_w=row(wts["rwkv_ln_w"]), rwkv_ln_b=row(wts["rwkv_ln_b"]), w_out=bf(wts["w_out"]),
        norm_ffn_g=row(wts["norm_ffn_g"]), peer_w_q_t=bf(wts["peer_w_q"]).T,
        peer_sub_keys=bf(wts["peer_sub_keys"]).reshape(2 * PEER_HEADS, PEER_N_KEYS, PEER_N_KEYS),
        peer_u=bf(wts["peer_u"]), peer_v_t=bf(wts["peer_v"]).T,
        ple_norm_g=row(wts["ple_norm_g"]), ple_w_gate=bf(wts["ple_w_gate"]), ple_b_gate=row(wts["ple_b_gate"]),
        ple_w_proj=bf(wts["ple_w_proj"]),
    )


def _trunk(x, p, wts, final_g):
    bsz, seq, _ = x.shape
    depth = p.shape[0]
    t = bsz * seq
    consts = _prep_consts()
    consts["cos"], consts["sin"] = _rope_tables(seq)
    consts["scan_masks"] = _scan_masks()
    h = x.reshape(t, D_MODEL)
    fg = final_g.reshape(1, D_MODEL)
    for i in range(depth):
        lw = _layer_weights(i, wts)
        h = _mixer(h, bsz, seq, lw, consts)
        h = _peer(h, lw)
        h = _ple(h, p[i].reshape(t, PLE_DIM), lw["ple_norm_g"], lw["ple_w_gate"], lw["ple_b_gate"],
                 lw["ple_w_proj"], fg, final=(i == depth - 1))
    return h.reshape(bsz, seq, D_MODEL)


def kernel(x_prompt, x_sample, p_prompt, p_sample, norm_mix_g, w_in, att_norm_g, rwkv_shift, rwkv_w0, rwkv_w2, rwkv_a0, rwkv_a2, rwkv_g2, rwkv_k_k, rwkv_k_a, rwkv_r_k, rwkv_ln_w, rwkv_ln_b, w_out, norm_ffn_g, peer_w_q, peer_sub_keys, peer_u, peer_v, ple_norm_g, ple_w_gate, ple_b_gate, ple_w_proj, final_norm_g):
    wts = dict(norm_mix_g=norm_mix_g, w_in=w_in, att_norm_g=att_norm_g, rwkv_shift=rwkv_shift, rwkv_w0=rwkv_w0,
               rwkv_w2=rwkv_w2, rwkv_a0=rwkv_a0, rwkv_a2=rwkv_a2, rwkv_g2=rwkv_g2, rwkv_k_k=rwkv_k_k,
               rwkv_k_a=rwkv_k_a, rwkv_r_k=rwkv_r_k, rwkv_ln_w=rwkv_ln_w, rwkv_ln_b=rwkv_ln_b, w_out=w_out,
               norm_ffn_g=norm_ffn_g, peer_w_q=peer_w_q, peer_sub_keys=peer_sub_keys, peer_u=peer_u,
               peer_v=peer_v, ple_norm_g=ple_norm_g, ple_w_gate=ple_w_gate, ple_b_gate=ple_b_gate,
               ple_w_proj=ple_w_proj)
    nb = x_prompt.shape[0]
    x = jnp.concatenate([x_prompt, x_sample], axis=0)
    p = jnp.concatenate([p_prompt, p_sample], axis=1)
    y = _trunk(x, p, wts, final_norm_g)
    return (y[:nb], y[nb:])
```

```python
import functools

import jax
import jax.numpy as jnp
import numpy as np
from jax import lax
from jax.experimental import pallas as pl
from jax.experimental.pallas import tpu as pltpu

F32 = jnp.float32
BF16 = jnp.bfloat16

D_MODEL = 1024
HEAD_DIM = 64
ATT_WIDTH = 512
ATT_HEADS = 8
RWKV_WIDTH = 512
RWKV_COLS = 1920
QKV_COLS = 3 * ATT_WIDTH
DILATIONS = (1, 4, 16)
HALF_WINDOW = 64
ROPE_THETA = 500000.0
ROPE_DIM = 16
NORM_EPS = 1e-6
LN_X_EPS = 64e-5
NEG_INF = -1e30
PEER_HEADS = 8
PEER_N_KEYS = 128
PEER_TOPK = 16
PLE_DIM = 256
CHUNK = 64
GROUP = 256

VMEM_LIMIT = 56 * 1024 * 1024


def _cparams(*sem, flags=None):
    return pltpu.CompilerParams(dimension_semantics=sem, vmem_limit_bytes=VMEM_LIMIT, flags=flags)


def _full(shape):
    nd = len(shape)
    return pl.BlockSpec(shape, lambda *_: (0,) * nd)


def _dot(a, b):
    return jnp.dot(a, b, preferred_element_type=F32)


def _dot_nt(a, b):
    return lax.dot_general(a, b, (((1,), (1,)), ((), ())), preferred_element_type=F32)


def _dot_tn(a, b):
    return lax.dot_general(a, b, (((0,), (0,)), ((), ())), preferred_element_type=F32)


def _split3(x):
    hi = x.astype(BF16)
    r1 = x - hi.astype(F32)
    mid = r1.astype(BF16)
    lo = (r1 - mid.astype(F32)).astype(BF16)
    return hi, mid, lo


def _dot_rexact(x, r_bf16):
    hi, mid, lo = _split3(x)
    return _dot(hi, r_bf16) + _dot(mid, r_bf16) + _dot(lo, r_bf16)


def _rms(x, g):
    return x * lax.rsqrt(jnp.mean(x * x, axis=-1, keepdims=True) + NORM_EPS) * g


def _sigmoid(x):
    return 1.0 / (1.0 + jnp.exp(-x))


LANES = 128
ATT_LANE_TILES = ATT_WIDTH // LANES


def _to_classes(t, stage, out_refs):
    rows = t.shape[0]
    for c in range(ATT_LANE_TILES):
        stage[c] = t[:, c * LANES:(c + 1) * LANES]
    for dil, ref in zip(DILATIONS, out_refs):
        if dil == 1:
            ref[...] = t.astype(ref.dtype)
            continue
        for r in range(dil):
            for c in range(ATT_LANE_TILES):
                lo = r * ATT_WIDTH + c * LANES
                ref[:, lo:lo + LANES] = stage[c, pl.ds(r, rows // dil, stride=dil), :].astype(ref.dtype)


def _from_classes(ref, dil, stage):
    if dil == 1:
        return ref[...]
    rows = ref.shape[0] * dil
    for r in range(dil):
        for c in range(ATT_LANE_TILES):
            lo = r * ATT_WIDTH + c * LANES
            stage[c, pl.ds(r, rows // dil, stride=dil), :] = ref[:, lo:lo + LANES]
    return jnp.concatenate([stage[c] for c in range(ATT_LANE_TILES)], axis=1)


def _inproj_kernel(h_ref, g_ref, w_ref, cos_ref, sin_ref, *refs):
    nd = len(DILATIONS)
    q_refs, k_refs, v_refs = refs[:nd], refs[nd:2 * nd], refs[2 * nd:3 * nd]
    u_ref, stage = refs[3 * nd:]
    xn = _rms(h_ref[...], g_ref[...]).astype(BF16)
    qkv = _dot(xn, w_ref[:, :QKV_COLS])
    u_ref[...] = _dot(xn, w_ref[:, QKV_COLS:])
    cos = cos_ref[...]
    sin = sin_ref[...]
    lane = lax.broadcasted_iota(jnp.int32, cos.shape, 1) % HEAD_DIM
    half = ROPE_DIM // 2

    def rot(t):
        partner = jnp.where(lane < half, pltpu.roll(t, ATT_WIDTH - half, 1), pltpu.roll(t, half, 1))
        return t * cos + partner * sin

    _to_classes(rot(qkv[:, :ATT_WIDTH]) * (HEAD_DIM ** -0.5), stage, q_refs)
    _to_classes(rot(qkv[:, ATT_WIDTH:2 * ATT_WIDTH]), stage, k_refs)
    _to_classes(qkv[:, 2 * ATT_WIDTH:], stage, v_refs)


def _inproj(h, g, w_bf16, cos_t, sin_t, seq, tm=512):
    t = h.shape[0]
    tm = min(tm, seq)
    nseq = seq // tm
    row = lambda i: (i, 0)
    pos = lambda i: (i % nseq, 0)
    cls_specs = [pl.BlockSpec((tm // d, d * ATT_WIDTH), row) for d in DILATIONS]
    cls_shapes = [jax.ShapeDtypeStruct((t // d, d * ATT_WIDTH), BF16) for d in DILATIONS]
    outs = pl.pallas_call(
        _inproj_kernel,
        grid=(t // tm,),
        in_specs=[pl.BlockSpec((tm, D_MODEL), row), _full((1, D_MODEL)), _full(w_bf16.shape),
                  pl.BlockSpec((tm, ATT_WIDTH), pos), pl.BlockSpec((tm, ATT_WIDTH), pos)],
        out_specs=cls_specs * 3 + [pl.BlockSpec((tm, RWKV_COLS), row)],
        out_shape=cls_shapes * 3 + [jax.ShapeDtypeStruct((t, RWKV_COLS), F32)],
        scratch_shapes=[pltpu.VMEM((ATT_LANE_TILES, tm, LANES), F32)],
        compiler_params=_cparams("parallel"),
        name="inproj",
    )(h, g, w_bf16, cos_t, sin_t)
    nd = len(DILATIONS)
    return outs[:nd], outs[nd:2 * nd], outs[2 * nd:3 * nd], outs[3 * nd]


def _rope_tables(seq):
    half = ROPE_DIM // 2
    inv_freq = ROPE_THETA ** (-jnp.arange(half, dtype=F32) / half)
    ang = jnp.arange(seq, dtype=F32)[:, None] * inv_freq[None, :]
    ones = jnp.ones((seq, HEAD_DIM - ROPE_DIM), F32)
    cos_h = jnp.concatenate([jnp.cos(ang), jnp.cos(ang), ones], axis=1)
    sin_h = jnp.concatenate([-jnp.sin(ang), jnp.sin(ang), 0.0 * ones], axis=1)
    return jnp.tile(cos_h, (1, ATT_HEADS)), jnp.tile(sin_h, (1, ATT_HEADS))


ATT_SUB = 128


def _attn_kernel(q_ref, kp_ref, kc_ref, kn_ref, vp_ref, vc_ref, vn_ref, o_ref, lse_ref,
                 kbuf, vbuf, *, length, tq):
    j = pl.program_id(2)
    hw = HALF_WINDOW
    kbuf[0:hw] = kp_ref[0]
    kbuf[hw:hw + tq] = kc_ref[0]
    kbuf[hw + tq:] = kn_ref[0]
    vbuf[0:hw] = vp_ref[0]
    vbuf[hw:hw + tq] = vc_ref[0]
    vbuf[hw + tq:] = vn_ref[0]
    nk = ATT_SUB + 2 * hw
    qi = lax.broadcasted_iota(jnp.int32, (ATT_SUB, nk), 0)
    ki = lax.broadcasted_iota(jnp.int32, (ATT_SUB, nk), 1)
    band = jnp.abs(ki - hw - qi) <= hw
    heads = [slice(h * HEAD_DIM, (h + 1) * HEAD_DIM) for h in range(ATT_HEADS)]
    blocks = [(s * ATT_SUB, c) for s in range(tq // ATT_SUB) for c in heads]
    sc = jnp.concatenate([_dot_nt(q_ref[0, q0:q0 + ATT_SUB, c], kbuf[q0:q0 + nk, c]) for q0, c in blocks], axis=0)
    valid = []
    for s in range(tq // ATT_SUB):
        kpos = j * tq + (s * ATT_SUB - hw) + ki
        valid += [band & (kpos >= 0) & (kpos < length)] * ATT_HEADS
    sc = jnp.where(jnp.concatenate(valid, axis=0), sc, NEG_INF)
    m = jnp.max(sc, axis=-1, keepdims=True)
    p = jnp.exp(sc - m)
    l = jnp.sum(p, axis=-1, keepdims=True)
    pb = p.astype(BF16)
    inv_l = 1.0 / l
    lse = m + jnp.log(l)
    for n, (q0, c) in enumerate(blocks):
        rows = slice(n * ATT_SUB, (n + 1) * ATT_SUB)
        o_ref[0, q0:q0 + ATT_SUB, c] = _dot(pb[rows], vbuf[q0:q0 + nk, c]) * inv_l[rows]
        lse_ref[0, q0:q0 + ATT_SUB, c] = jnp.broadcast_to(lse[rows], (ATT_SUB, HEAD_DIM))


def _attn_pattern(q, k, v, bsz, seq, dil):
    length = seq // dil
    tq = min(length, 512)
    nq = length // tq
    nh = length // HALF_WINDOW
    rq = tq // HALF_WINDOW
    view = lambda a: a.reshape(bsz, length, dil * ATT_WIDTH)
    cur = pl.BlockSpec((1, tq, ATT_WIDTH), lambda b, r, j: (b, j, r))
    prev = pl.BlockSpec((1, HALF_WINDOW, ATT_WIDTH), lambda b, r, j: (b, jnp.maximum(j * rq - 1, 0), r))
    nxt = pl.BlockSpec((1, HALF_WINDOW, ATT_WIDTH), lambda b, r, j: (b, jnp.minimum((j + 1) * rq, nh - 1), r))
    out_sd = jax.ShapeDtypeStruct((bsz, length, dil * ATT_WIDTH), F32)
    o, lse = pl.pallas_call(
        functools.partial(_attn_kernel, length=length, tq=tq),
        grid=(bsz, dil, nq),
        in_specs=[cur, prev, cur, nxt, prev, cur, nxt],
        out_specs=[cur, cur],
        out_shape=[out_sd, out_sd],
        scratch_shapes=[pltpu.VMEM((tq + 2 * HALF_WINDOW, ATT_WIDTH), BF16)] * 2,
        compiler_params=_cparams("parallel", "parallel", "parallel"),
        name=f"attn_d{dil}",
    )(view(q), view(k), view(k), view(k), view(v), view(v), view(v))
    flat = (bsz * length, dil * ATT_WIDTH)
    return o.reshape(flat), lse.reshape(flat)


R_TILE = 256
R_COLS = dict(r=0, k=512, v=1024, wd=1536, ad=1664, gd=1792)
DECAY_LOG_SCALE = float(np.exp(-0.5))


def _rwkv_prep_kernel(uc_ref, up_ref, un_ref, shift_ref, w0_ref, w2_ref, a0_ref, a2_ref, g2_ref,
                      kk_ref, ka_ref, rk_ref, bd_ref, tri_ref, sel_ref,
                      rt_ref, at_ref, bt_ref, kt_ref, bh_ref, kh_ref, v_ref, wdec_ref, bonus_ref, g_ref):
    j = pl.program_id(1)
    nj = pl.num_programs(1)
    tr = R_TILE
    cur = uc_ref[0]
    row = lax.broadcasted_iota(jnp.int32, cur.shape, 0)
    prev_edge = up_ref[0, 7:8, :] * (j > 0).astype(F32)
    next_edge = un_ref[0, 0:1, :] * (j < nj - 1).astype(F32)
    prev = jnp.where(row == 0, prev_edge, pltpu.roll(cur, 1, 0))
    nxt = jnp.where(row == tr - 1, next_edge, pltpu.roll(cur, tr - 1, 0))
    us = shift_ref[0:1, :] * prev + shift_ref[1:2, :] * cur + shift_ref[2:3, :] * nxt

    w = RWKV_WIDTH
    r = us[:, R_COLS["r"]:R_COLS["r"] + w]
    k = us[:, R_COLS["k"]:R_COLS["k"] + w]
    v = us[:, R_COLS["v"]:R_COLS["v"] + w]
    gd = us[:, R_COLS["gd"]:]
    bd = bd_ref[...]

    g_ref[...] = _dot(_sigmoid(gd).astype(BF16), g2_ref[...])
    v_ref[...] = v.astype(BF16)
    kk = k * kk_ref[...]
    kk = kk / jnp.maximum(jnp.sqrt(_dot_rexact(kk * kk, bd)), 1e-12)

    nchunk = tr // CHUNK
    ksum = jnp.zeros_like(k)
    for d in range(2):
        wd = us[:, R_COLS["wd"] + 64 * d:R_COLS["wd"] + 64 * (d + 1)]
        ad = us[:, R_COLS["ad"] + 64 * d:R_COLS["ad"] + 64 * (d + 1)]
        z = w0_ref[d:d + 1, :] + _dot(jnp.tanh(wd).astype(BF16), w2_ref[d])
        lw = -DECAY_LOG_SCALE * _sigmoid(z)
        a = _sigmoid(a0_ref[d:d + 1, :] + _dot(ad.astype(BF16), a2_ref[d]))
        kd = k * (1.0 + (a - 1.0) * ka_ref[...])
        ksum = ksum + kd
        b = kk * a
        lw3 = _split3(lw)
        sum3 = lambda l: _dot(l, lw3[0]) + _dot(l, lw3[1]) + _dot(l, lw3[2])
        c = sum3(tri_ref[d])
        tot = sum3(sel_ref[...])
        cend = jnp.concatenate(
            [jnp.broadcast_to(tot[i:i + 1, :], (CHUNK, w)) for i in range(nchunk)], axis=0)
        e_c = jnp.exp(c)
        e_nc = jnp.exp(-c)
        e_rem = jnp.exp(cend - c)
        rt_ref[d] = (r * e_c).astype(BF16)
        at_ref[d] = (-kk * jnp.exp(c - lw)).astype(BF16)
        bt_ref[d] = (b * e_nc).astype(BF16)
        kt_ref[d] = (kd * e_nc).astype(BF16)
        bh_ref[d] = (b * e_rem).astype(BF16)
        kh_ref[d] = (kd * e_rem).astype(BF16)
        wdec = jnp.exp(tot)
        for i in range(nchunk):
            wdec_ref[d, 0, i] = wdec[i:i + 1, :]
    bonus_ref[...] = _dot_rexact(r * ksum * rk_ref[...], bd) * v


def _prep_consts():
    tr = R_TILE
    idx = np.arange(tr)
    same = (idx[:, None] // CHUNK) == (idx[None, :] // CHUNK)
    tri = np.stack([same & (idx[:, None] >= idx[None, :]), same & (idx[:, None] <= idx[None, :])])
    sel = (np.arange(8)[:, None] == (idx[None, :] // CHUNK))
    hd = np.arange(RWKV_WIDTH) // HEAD_DIM
    bd = hd[:, None] == hd[None, :]
    as_bf16 = lambda m: jnp.asarray(m.astype(np.float32), dtype=BF16)
    return dict(tri=as_bf16(tri), sel=as_bf16(sel), bd512=as_bf16(bd))


def _rwkv_prep(u, bsz, seq, shift_w, w0, w2, a0, a2, g2, k_k, k_a, r_k, consts):
    tr = R_TILE
    nj = seq // tr
    t = bsz * seq
    w = RWKV_WIDTH
    u3 = u.reshape(bsz, seq, RWKV_COLS)
    rows8 = tr // 8
    cur = pl.BlockSpec((1, tr, RWKV_COLS), lambda b, j: (b, j, 0))
    prev = pl.BlockSpec((1, 8, RWKV_COLS), lambda b, j: (b, jnp.maximum(j * rows8 - 1, 0), 0))
    nxt = pl.BlockSpec((1, 8, RWKV_COLS), lambda b, j: (b, jnp.minimum((j + 1) * rows8, seq // 8 - 1), 0))
    tok2 = pl.BlockSpec((2, tr, w), lambda b, j: (0, b * nj + j, 0))
    tok = pl.BlockSpec((tr, w), lambda b, j: (b * nj + j, 0))
    nchunk = tr // CHUNK
    wdec_spec = pl.BlockSpec((2, 1, nchunk, 1, w), lambda b, j: (0, b, j, 0, 0))
    small = [shift_w, w0, w2, a0, a2, g2, k_k, k_a, r_k, consts["bd512"], consts["tri"], consts["sel"]]
    sd2 = jax.ShapeDtypeStruct((2, t, w), BF16)
    return pl.pallas_call(
        _rwkv_prep_kernel,
        grid=(bsz, nj),
        in_specs=[cur, prev, nxt] + [_full(a.shape) for a in small],
        out_specs=[tok2] * 6 + [tok, wdec_spec, tok, tok],
        out_shape=[sd2] * 6 + [jax.ShapeDtypeStruct((t, w), BF16),
                               jax.ShapeDtypeStruct((2, bsz, seq // CHUNK, 1, w), F32),
                               jax.ShapeDtypeStruct((t, w), F32), jax.ShapeDtypeStruct((t, w), F32)],
        compiler_params=_cparams("parallel", "parallel"),
        name="rwkv_prep",
    )(u3, u3, u3, *small)


S_TILE = 256
LEVELS = (2, 4, 8, 16, 32)
N_DIR_OPERANDS = 6


def _rwkv_scan_kernel(*refs):
    nop = N_DIR_OPERANDS
    dir_refs = refs[:2 * nop]
    vf_ref, vb_ref, wf_ref, wb_ref, mask_ref, yf_ref, yb_ref, state = refs[2 * nop:]
    j = pl.program_id(1)

    @pl.when(j == 0)
    def _():
        state[...] = jnp.zeros_like(state)

    nchunk = S_TILE // CHUNK
    nh = GROUP // HEAD_DIM
    ng = RWKV_WIDTH // GROUP
    bdm = mask_ref[0]
    bdm_f32 = bdm.astype(F32)
    eye = mask_ref[1].astype(F32)
    tile4 = lambda x: jnp.concatenate([x] * nh, axis=0)
    io = ((vf_ref, wf_ref, yf_ref), (vb_ref, wb_ref, yb_ref))
    order = [(d, g, ci if d == 0 else nchunk - 1 - ci)
             for ci in range(nchunk) for g in range(ng) for d in range(2)]

    pre = {}
    for d, g, c in order:
        rt_ref, at_ref, bt_ref, kt_ref, bh_ref, kh_ref = [dir_refs[2 * i + d] for i in range(nop)]
        rows = (0, slice(c * CHUNK, (c + 1) * CHUNK), slice(g * GROUP, (g + 1) * GROUP))
        a_bd = tile4(at_ref[rows]) * bdm
        r_bd = tile4(rt_ref[rows]) * bdm
        v_bd = tile4(io[d][0][rows]) * bdm
        ar = jnp.concatenate([a_bd, r_bd], axis=0)
        ar_b = _dot_nt(ar, tile4(bt_ref[rows]))
        ar_k = _dot_nt(ar, tile4(kt_ref[rows]))
        strict = mask_ref[2 + d]
        incl = mask_ref[4 + d]
        a_ab = ar_b[:GROUP].astype(BF16) * strict
        a_ak = ar_k[:GROUP].astype(BF16) * strict
        a_r = jnp.concatenate([ar_b[GROUP:].astype(BF16) * incl, ar_k[GROUP:].astype(BF16) * incl], axis=1)
        hat = jnp.concatenate([tile4(bh_ref[rows]), tile4(kh_ref[rows])], axis=0)
        pre[d, g, c] = dict(a_bd=a_bd, r_bd=r_bd, v_bd=v_bd, a_ab=a_ab, a_ak=a_ak, a_r=a_r, hat=hat,
                            tinv=eye + (a_ab * mask_ref[6 + d]).astype(F32))
    for li in range(len(LEVELS)):
        for key in order:
            p = pre[key]
            lm = p["a_ab"] * mask_ref[8 + 2 * li + key[0]]
            tb = p["tinv"].astype(BF16)
            p["tinv"] = p["tinv"] + _dot(tb, _dot(lm, tb).astype(BF16))
    s = {(d, g): state[d, g] for d in range(2) for g in range(ng)}
    for d, g, c in order:
        p = pre[d, g, c]
        w_ref, y_ref = io[d][1], io[d][2]
        cols = slice(g * GROUP, (g + 1) * GROUP)
        s_bf = s[d, g].astype(BF16)
        rhs = _dot_nt(p["a_bd"], s_bf) + _dot(p["a_ak"], p["v_bd"])
        u_bf = _dot(p["tinv"].astype(BF16), rhs.astype(BF16)).astype(BF16)
        uv = jnp.concatenate([u_bf, p["v_bd"]], axis=0)
        y = _dot_nt(p["r_bd"], s_bf) + _dot(p["a_r"], uv)
        y_nat = y[0:CHUNK]
        for hh in range(1, nh):
            y_nat = y_nat + y[hh * CHUNK:(hh + 1) * CHUNK]
        y_ref[0, c * CHUNK:(c + 1) * CHUNK, cols] = y_nat
        s[d, g] = s[d, g] * w_ref[0, 0, c, :, cols] + _dot_tn(uv, p["hat"]) * bdm_f32
    for d in range(2):
        for g in range(ng):
            state[d, g] = s[d, g]


def _scan_masks():
    n = GROUP
    idx = np.arange(n)
    head = idx // CHUNK
    t = idx % CHUNK
    same = head[:, None] == head[None, :]
    tt, ss = t[:, None], t[None, :]
    masks = [same, np.eye(n, dtype=bool)]
    masks += [same & (tt > ss), same & (tt < ss)]
    masks += [same & (tt >= ss), same & (tt <= ss)]
    for m in (1,) + LEVELS:
        blk = same & (tt // (2 * m) == ss // (2 * m))
        lower = blk & ((tt // m) % 2 == 1) & ((ss // m) % 2 == 0)
        upper = blk & ((tt // m) % 2 == 0) & ((ss // m) % 2 == 1)
        masks += [lower, upper]
    return jnp.asarray(np.stack(masks).astype(np.float32), dtype=BF16)


def _rwkv_scan(dir_ops, v, wdec, bsz, seq, masks):
    st = S_TILE
    nj = seq // st
    ng = RWKV_WIDTH // GROUP
    w = RWKV_WIDTH
    t = bsz * seq
    nchunk = st // CHUNK
    blk = (1, st, w)
    fwd = lambda d: pl.BlockSpec(blk, lambda b, j: (d, b * nj + j, 0))
    bwd = lambda d: pl.BlockSpec(blk, lambda b, j: (d, b * nj + (nj - 1 - j), 0))
    wf = pl.BlockSpec((1, 1, nchunk, 1, w), lambda b, j: (0, b, j, 0, 0))
    wb = pl.BlockSpec((1, 1, nchunk, 1, w), lambda b, j: (1, b, nj - 1 - j, 0, 0))
    v3 = v.reshape(1, t, w)
    in_specs, args = [], []
    for arr in dir_ops:
        in_specs += [fwd(0), bwd(1)]
        args += [arr, arr]
    in_specs += [fwd(0), bwd(0), wf, wb, _full(masks.shape)]
    args += [v3, v3, wdec, wdec, masks]
    sd = jax.ShapeDtypeStruct((1, t, w), F32)
    yf, yb = pl.pallas_call(
        _rwkv_scan_kernel,
        grid=(bsz, nj),
        in_specs=in_specs,
        out_specs=[fwd(0), bwd(0)],
        out_shape=[sd, sd],
        scratch_shapes=[pltpu.VMEM((2, ng, GROUP, GROUP), F32)],
        compiler_params=_cparams("parallel", "arbitrary"),
        name="rwkv_scan",
    )(*args)
    return yf.reshape(t, w), yb.reshape(t, w)


def _outproj_kernel(h_ref, o1_ref, o2_ref, o3_ref, l1_ref, l2_ref, l3_ref, attg_ref,
                    yf_ref, yb_ref, bonus_ref, g_ref, lnw_ref, lnb_ref, bd_ref, w_ref, out_ref, stage):
    bd = bd_ref[...]
    inv_n = 1.0 / HEAD_DIM
    nd = len(DILATIONS)
    o1, o2, o3 = [_from_classes(r, d, stage.at[i])
                  for i, (r, d) in enumerate(zip((o1_ref, o2_ref, o3_ref), DILATIONS))]
    l1, l2, l3 = [_from_classes(r, d, stage.at[nd + i])
                  for i, (r, d) in enumerate(zip((l1_ref, l2_ref, l3_ref), DILATIONS))]
    m = jnp.maximum(jnp.maximum(l1, l2), l3)
    w1, w2, w3 = jnp.exp(l1 - m), jnp.exp(l2 - m), jnp.exp(l3 - m)
    o = (w1 * o1 + w2 * o2 + w3 * o3) / (w1 + w2 + w3)
    att = o * lax.rsqrt(_dot_rexact(o * o, bd) * inv_n + NORM_EPS) * attg_ref[...]
    y = yf_ref[...] + yb_ref[...]
    yc = y - _dot_rexact(y, bd) * inv_n
    var = _dot_rexact(yc * yc, bd) * inv_n
    yn = yc * lax.rsqrt(var + LN_X_EPS) * lnw_ref[...] + lnb_ref[...]
    rw = (yn + bonus_ref[...]) * g_ref[...]
    out_ref[...] = (h_ref[...] + _dot(att.astype(BF16), w_ref[:ATT_WIDTH, :])
                    + _dot(rw.astype(BF16), w_ref[ATT_WIDTH:, :]))


def _outproj(h, att_parts, att_g, yf, yb, bonus, g, ln_w, ln_b, bd512, w_out_bf16, tm=256):
    t = h.shape[0]
    row = lambda i: (i, 0)
    half = pl.BlockSpec((tm, ATT_WIDTH), row)
    cls = [pl.BlockSpec((tm // d, d * ATT_WIDTH), row) for d in DILATIONS]
    (o1, l1), (o2, l2), (o3, l3) = att_parts
    vec = _full((1, ATT_WIDTH))
    return pl.pallas_call(
        _outproj_kernel,
        grid=(t // tm,),
        in_specs=[pl.BlockSpec((tm, D_MODEL), row)] + cls * 2 + [vec] + [half] * 4 + [vec, vec,
                  _full(bd512.shape), _full(w_out_bf16.shape)],
        out_specs=pl.BlockSpec((tm, D_MODEL), row),
        out_shape=jax.ShapeDtypeStruct((t, D_MODEL), F32),
        scratch_shapes=[pltpu.VMEM((2 * len(DILATIONS), ATT_LANE_TILES, tm, LANES), F32)],
        compiler_params=_cparams("parallel"),
        name="outproj",
    )(h, o1, o2, o3, l1, l2, l3, att_g, yf, yb, bonus, g, ln_w, ln_b, bd512, w_out_bf16)


ROUTE_TILE = 256
ROUTE_HEADS_PER_TRIP = 8


def _oddeven_merge_sort(n):
    pairs = []
    p = 1
    while p < n:
        k = p
        while k >= 1:
            for j in range(k % p, n - k, 2 * k):
                for i in range(min(k, n - j - k)):
                    if (i + j) // (2 * p) == (i + j + k) // (2 * p):
                        pairs.append((i + j, i + j + k))
            k //= 2
        p *= 2
    return tuple(pairs)


SUBLANES = 8


def _top_values(s, k, on_round):
    nv = s.shape[0] // SUBLANES
    x = [s[SUBLANES * v:SUBLANES * (v + 1), :] for v in range(nv)]
    for a, b in _oddeven_merge_sort(nv):
        x[a], x[b] = jnp.maximum(x[a], x[b]), jnp.minimum(x[a], x[b])
    for r in range(k):
        m = jnp.max(x[0], axis=0, keepdims=True)
        on_round(r, m)
        if r + 1 < k:
            hit = x[0] == m
            depth = min(nv - 1, k - 1 - r)
            for v in range(depth):
                x[v] = jnp.where(hit, x[v + 1], x[v])
            x[depth] = jnp.where(hit, NEG_INF, x[depth])


def _peer_route_kernel(h_ref, g_ref, wq_ref, sk_ref, xn_ref, rank1_ref, e1_ref, cnt_ref, e0_ref, qt):
    tp = ROUTE_TILE
    nk = PEER_N_KEYS
    k = PEER_TOPK
    xn = _rms(h_ref[...], g_ref[...]).astype(BF16)
    xn_ref[...] = xn
    qt[...] = _dot_nt(wq_ref[...], xn)
    row16 = lax.broadcasted_iota(jnp.int32, (k, tp), 0)

    def one_head(h):
        tops, scores = [], []
        for p in range(2):
            hp = 2 * h + p
            q_hp = qt[pl.ds(pl.multiple_of(hp * nk, nk), nk), :].astype(BF16)
            s = _dot(sk_ref[hp], q_hp)
            scores.append(s)
            box = [jnp.full((k, tp), NEG_INF, F32)]

            def on_round(r, m, box=box):
                box[0] = jnp.where(row16 == r, m, box[0])

            _top_values(s, k, on_round)
            tops.append(box[0])
        t0, t1 = tops
        half = k // 2
        cand = jnp.concatenate([t0[0:1, :] + t1] + [t0[a:a + 1, :] + t1[0:half, :] for a in range(1, half)]
                               + [t0[half:, :] + t1[b:b + 1, :] for b in range(half - 1)], axis=0)
        top = t0[0:1, :] + t1[0:1, :]
        acc = [jnp.zeros((1, tp), F32), top]

        def on_cand(r, m, acc=acc):
            acc[0] = acc[0] + jnp.exp(m - top)
            acc[1] = m

        _top_values(cand, k, on_cand)
        z, thr = acc
        n = jnp.zeros((k, tp), F32)
        for b in range(k):
            n = n + jnp.where(t0 + t1[b:b + 1, :] >= thr, 1.0, 0.0)
        cnt = jnp.zeros((nk, tp), F32)
        for a in range(k):
            cnt = jnp.where(scores[0] == t0[a:a + 1, :], n[a:a + 1, :], cnt)
        rank1 = jnp.full((nk, tp), float(k), F32)
        for b in reversed(range(k)):
            rank1 = jnp.where(scores[1] == t1[b:b + 1, :], float(b), rank1)
        rank1_ref[h] = rank1.astype(BF16)
        e1_ref[h] = jnp.exp(scores[1] - t1[0:1, :]).astype(BF16)
        cnt_ref[h] = cnt
        e0_ref[h] = jnp.exp(scores[0] - t0[0:1, :]) / z

    def head_group(i, carry):
        for hh in range(ROUTE_HEADS_PER_TRIP):
            one_head(ROUTE_HEADS_PER_TRIP * i + hh)
        return carry

    lax.fori_loop(0, PEER_HEADS // ROUTE_HEADS_PER_TRIP, head_group, 0)


def _peer_route(h, g, wq_t_bf16, sub_keys_bf16):
    t = h.shape[0]
    tp = ROUTE_TILE
    nk = PEER_N_KEYS
    nh = PEER_HEADS
    row = lambda i: (i, 0)
    lanes = lambda i: (0, 0, i)
    spec = pl.BlockSpec((nh, nk, tp), lanes)
    sd = lambda dt: jax.ShapeDtypeStruct((nh, nk, t), dt)
    return pl.pallas_call(
        _peer_route_kernel,
        grid=(t // tp,),
        in_specs=[pl.BlockSpec((tp, D_MODEL), row), _full((1, D_MODEL)), _full(wq_t_bf16.shape),
                  _full(sub_keys_bf16.shape)],
        out_specs=[pl.BlockSpec((tp, D_MODEL), row), spec, spec, spec, spec],
        out_shape=[jax.ShapeDtypeStruct((t, D_MODEL), BF16), sd(BF16), sd(BF16), sd(F32), sd(F32)],
        scratch_shapes=[pltpu.VMEM((nh * 2 * nk, tp), F32)],
        compiler_params=_cparams("parallel"),
        name="peer_route",
    )(h, g, wq_t_bf16, sub_keys_bf16)


E_TILE = 2048
DENSE_TILE = 512


BF16_ROWS = 16


def _rows_bf16(row, nrows):
    one = jnp.broadcast_to(row, (BF16_ROWS, row.shape[1])).astype(BF16)
    return jnp.concatenate([one] * (nrows // BF16_ROWS), axis=0)


def _peer_dense_kernel(xn_ref, u_ref, vt_ref, rank1_ref, e1_ref, cnt_ref, e0_ref, h_ref, out_ref, acc):
    e = pl.program_id(1)
    nk = PEER_N_KEYS
    tp = xn_ref.shape[0]

    @pl.when(e == 0)
    def _():
        acc[...] = jnp.zeros_like(acc)

    gates = []
    for ii in range(E_TILE // nk):
        i = e * (E_TILE // nk) + ii
        gate = jnp.zeros((nk, tp), BF16)
        for h in range(PEER_HEADS):
            cnt_row = _rows_bf16(cnt_ref[h, pl.ds(i, 1), :], nk)
            e0_row = _rows_bf16(e0_ref[h, pl.ds(i, 1), :], nk)
            gate = gate + jnp.where(rank1_ref[h] < cnt_row, e1_ref[h], jnp.zeros((), BF16)) * e0_row
        gates.append(gate)
    hid = _dot_nt(u_ref[...], xn_ref[...])
    hb = hid.astype(BF16)
    act = (0.5 * hb) * (1.0 + lax.erf(hb * (2.0 ** -0.5)))
    acc[...] += _dot(vt_ref[...], act * jnp.concatenate(gates, axis=0))

    @pl.when(e == pl.num_programs(1) - 1)
    def _():
        out_ref[...] = h_ref[...] + acc[...].T


def _peer_dense(h, xn, u_bf16, vt_bf16, rank1, e1, cnt, e0):
    t = h.shape[0]
    tp = min(DENSE_TILE, t)
    ne = u_bf16.shape[0]
    nh, nk = PEER_HEADS, PEER_N_KEYS
    tok = lambda i, e: (i, 0)
    lanes = lambda i, e: (0, 0, i)
    spec = pl.BlockSpec((nh, nk, tp), lanes)
    return pl.pallas_call(
        _peer_dense_kernel,
        grid=(t // tp, ne // E_TILE),
        in_specs=[pl.BlockSpec((tp, D_MODEL), tok), pl.BlockSpec((E_TILE, D_MODEL), lambda i, e: (e, 0)),
                  pl.BlockSpec((D_MODEL, E_TILE), lambda i, e: (0, e)), spec, spec, spec, spec,
                  pl.BlockSpec((tp, D_MODEL), tok)],
        out_specs=pl.BlockSpec((tp, D_MODEL), tok),
        out_shape=jax.ShapeDtypeStruct((t, D_MODEL), F32),
        scratch_shapes=[pltpu.VMEM((D_MODEL, tp), F32)],
        compiler_params=_cparams("parallel", "arbitrary"),
        name="peer_dense",
    )(xn, u_bf16, vt_bf16, rank1, e1, cnt, e0, h)


def _ple_kernel(h_ref, p_ref, g_ref, wg_ref, b_ref, wp_ref, fg_ref, out_ref, *, final):
    h = h_ref[...]
    gate = _sigmoid(_dot(_rms(h, g_ref[...]).astype(BF16), wg_ref[...]) + b_ref[...])
    out = h + gate * _dot(p_ref[...].astype(BF16), wp_ref[...])
    out_ref[...] = _rms(out, fg_ref[...]) if final else out


def _ple(h, p, g, wg_bf16, b, wp_bf16, final_g, final, tm=512):
    t = h.shape[0]
    row = lambda i: (i, 0)
    vec = _full((1, D_MODEL))
    return pl.pallas_call(
        functools.partial(_ple_kernel, final=final),
        grid=(t // tm,),
        in_specs=[pl.BlockSpec((tm, D_MODEL), row), pl.BlockSpec((tm, PLE_DIM), row), vec,
                  _full(wg_bf16.shape), vec, _full(wp_bf16.shape), vec],
        out_specs=pl.BlockSpec((tm, D_MODEL), row),
        out_shape=jax.ShapeDtypeStruct((t, D_MODEL), F32),
        compiler_params=_cparams("parallel"),
        name="ple_final" if final else "ple",
    )(h, p, g, wg_bf16, b, wp_bf16, final_g)


def _mixer(h, bsz, seq, lw, consts):
    qs, ks, vs, u = _inproj(h, lw["norm_mix_g"], lw["w_in"], consts["cos"], consts["sin"], seq)
    att_parts = [_attn_pattern(q, k, v, bsz, seq, dil) for q, k, v, dil in zip(qs, ks, vs, DILATIONS)]
    prep = _rwkv_prep(u, bsz, seq, lw["rwkv_shift"], lw["rwkv_w0"], lw["rwkv_w2"], lw["rwkv_a0"],
                      lw["rwkv_a2"], lw["rwkv_g2"], lw["rwkv_k_k"], lw["rwkv_k_a"], lw["rwkv_r_k"], consts)
    dir_ops, (rv, wdec, bonus, g) = prep[:N_DIR_OPERANDS], prep[N_DIR_OPERANDS:]
    yf, yb = _rwkv_scan(dir_ops, rv, wdec, bsz, seq, consts["scan_masks"])
    return _outproj(h, att_parts, lw["att_norm_g"], yf, yb, bonus, g, lw["rwkv_ln_w"], lw["rwkv_ln_b"],
                    consts["bd512"], lw["w_out"])


def _peer(h, lw):
    xn, rank1, e1, cnt, e0 = _peer_route(h, lw["norm_ffn_g"], lw["peer_w_q_t"], lw["peer_sub_keys"])
    return _peer_dense(h, xn, lw["peer_u"], lw["peer_v_t"], rank1, e1, cnt, e0)


def _layer_weights(i, wts):
    row = lambda a: a[i].reshape(1, -1)
    bf = lambda a: a[i].astype(BF16)
    return dict(
        norm_mix_g=row(wts["norm_mix_g"]), w_in=bf(wts["w_in"]), att_norm_g=row(wts["att_norm_g"]),
        rwkv_shift=wts["rwkv_shift"][i], rwkv_w0=wts["rwkv_w0"][i], rwkv_w2=bf(wts["rwkv_w2"]),
        rwkv_a0=wts["rwkv_a0"][i], rwkv_a2=bf(wts["rwkv_a2"]), rwkv_g2=bf(wts["rwkv_g2"]),
        rwkv_k_k=row(wts["rwkv_k_k"]), rwkv_k_a=row(wts["rwkv_k_a"]), rwkv_r_k=row(wts["rwkv_r_k"]),
        rwkv_ln_w=row(wts["rwkv_ln_w"]), rwkv_ln_b=row(wts["rwkv_ln_b"]), w_out=bf(wts["w_out"]),
        norm_ffn_g=row(wts["norm_ffn_g"]), peer_w_q_t=bf(wts["peer_w_q"]).T,
        peer_sub_keys=bf(wts["peer_sub_keys"]).reshape(2 * PEER_HEADS, PEER_N_KEYS, PEER_N_KEYS),
        peer_u=bf(wts["peer_u"]), peer_v_t=bf(wts["peer_v"]).T,
        ple_norm_g=row(wts["ple_norm_g"]), ple_w_gate=bf(wts["ple_w_gate"]), ple_b_gate=row(wts["ple_b_gate"]),
        ple_w_proj=bf(wts["ple_w_proj"]),
    )


def _trunk(x, p, wts, final_g):
    bsz, seq, _ = x.shape
    depth = p.shape[0]
    t = bsz * seq
    consts = _prep_consts()
    consts["cos"], consts["sin"] = _rope_tables(seq)
    consts["scan_masks"] = _scan_masks()
    h = x.reshape(t, D_MODEL)
    fg = final_g.reshape(1, D_MODEL)
    for i in range(depth):
        lw = _layer_weights(i, wts)
        h = _mixer(h, bsz, seq, lw, consts)
        h = _peer(h, lw)
        h = _ple(h, p[i].reshape(t, PLE_DIM), lw["ple_norm_g"], lw["ple_w_gate"], lw["ple_b_gate"],
                 lw["ple_w_proj"], fg, final=(i == depth - 1))
    return h.reshape(bsz, seq, D_MODEL)


def kernel(x_prompt, x_sample, p_prompt, p_sample, norm_mix_g, w_in, att_norm_g, rwkv_shift, rwkv_w0, rwkv_w2, rwkv_a0, rwkv_a2, rwkv_g2, rwkv_k_k, rwkv_k_a, rwkv_r_k, rwkv_ln_w, rwkv_ln_b, w_out, norm_ffn_g, peer_w_q, peer_sub_keys, peer_u, peer_v, ple_norm_g, ple_w_gate, ple_b_gate, ple_w_proj, final_norm_g):
    wts = dict(norm_mix_g=norm_mix_g, w_in=w_in, att_norm_g=att_norm_g, rwkv_shift=rwkv_shift, rwkv_w0=rwkv_w0,
               rwkv_w2=rwkv_w2, rwkv_a0=rwkv_a0, rwkv_a2=rwkv_a2, rwkv_g2=rwkv_g2, rwkv_k_k=rwkv_k_k,
               rwkv_k_a=rwkv_k_a, rwkv_r_k=rwkv_r_k, rwkv_ln_w=rwkv_ln_w, rwkv_ln_b=rwkv_ln_b, w_out=w_out,
               norm_ffn_g=norm_ffn_g, peer_w_q=peer_w_q, peer_sub_keys=peer_sub_keys, peer_u=peer_u,
               peer_v=peer_v, ple_norm_g=ple_norm_g, ple_w_gate=ple_w_gate, ple_b_gate=ple_b_gate,
               ple_w_proj=ple_w_proj)
    nb = x_prompt.shape[0]
    x = jnp.concatenate([x_prompt, x_sample], axis=0)
    p = jnp.concatenate([p_prompt, p_sample], axis=1)
    y = _trunk(x, p, wts, final_norm_g)
    return (y[:nb], y[nb:])
```
